```python
import math
import jax, jax.numpy as jnp
from jax import lax
import numpy as np

D_MODEL = 1024
BATCH = 8
SEQ = 2048
DEPTH = 1

PLE_DIM = 256
EPS = 1e-6
NEG_INF = -1e30
N_Q_HEADS = 8
N_KV_HEADS = 2
HEAD_DIM = 64
Q_PER_KV = N_Q_HEADS // N_KV_HEADS
WINDOW = 128
BLOCK = 128
LRU_WIDTH = D_MODEL
LRU_BLOCKS = 8
LRU_BLOCK_DIM = LRU_WIDTH // LRU_BLOCKS
CONV_WIDTH = 4
LRU_C = 8.0
PEER_HEADS = 8
PEER_KEYS = 128
PEER_N_EXPERTS = PEER_KEYS * PEER_KEYS
PEER_QDIM = 256
PEER_HALF = PEER_QDIM // 2
PEER_TOPK = 16
PEER_CHUNK = 128
Q_W = N_Q_HEADS * HEAD_DIM
KV_W = N_KV_HEADS * HEAD_DIM
SPLIT_POINTS = (Q_W, Q_W + KV_W, Q_W + 2 * KV_W, Q_W + 2 * KV_W + LRU_WIDTH,
                Q_W + 2 * KV_W + 2 * LRU_WIDTH, Q_W + 2 * KV_W + 2 * LRU_WIDTH + D_MODEL)
IN_WIDTH = Q_W + 2 * KV_W + 2 * LRU_WIDTH + 2 * D_MODEL

kernel_name = "hybrid_swa_rglru_peer_block"


def rms_norm(x, g):
    xf = x.astype(jnp.float32)
    y = xf * lax.rsqrt(jnp.mean(xf * xf, axis=-1, keepdims=True) + EPS)
    return (y * g.astype(jnp.float32)).astype(x.dtype)


def sliding_window_attention(q, k, v, sink):
    B, S = q.shape[0], q.shape[1]
    nb = S // BLOCK
    qb = q.reshape(B, nb, BLOCK, N_KV_HEADS, Q_PER_KV, HEAD_DIM)
    kb = k.reshape(B, nb, BLOCK, N_KV_HEADS, HEAD_DIM)
    vb = v.reshape(B, nb, BLOCK, N_KV_HEADS, HEAD_DIM)
    pad = ((0, 0), (1, 0), (0, 0), (0, 0), (0, 0))
    kk = jnp.concatenate([jnp.pad(kb, pad)[:, :-1], kb], axis=2)
    vv = jnp.concatenate([jnp.pad(vb, pad)[:, :-1], vb], axis=2)
    scores = jnp.einsum('bnqgrd,bnkgd->bgrnqk', qb, kk).astype(jnp.float32) * (HEAD_DIM ** -0.5)
    qi = jnp.arange(BLOCK)
    kj = jnp.arange(2 * BLOCK)
    dist = (qi[:, None] + BLOCK - kj[None, :]).astype(jnp.float32)
    key_pos = jnp.arange(nb)[:, None] * BLOCK + kj[None, :] - BLOCK
    mask = ((dist >= 0) & (dist < WINDOW))[None] & (key_pos >= 0)[:, None, :]
    slopes = (2.0 ** (-8.0 * jnp.arange(1, N_Q_HEADS + 1, dtype=jnp.float32) / N_Q_HEADS))
    slopes = slopes.reshape(N_KV_HEADS, Q_PER_KV)[:, :, None, None, None]
    scores = jnp.where(mask, scores - slopes * dist, NEG_INF)
    sink_logit = sink.astype(jnp.float32).reshape(N_KV_HEADS, Q_PER_KV)[None, :, :, None, None, None]
    sink_logit = jnp.broadcast_to(sink_logit, scores.shape[:-1] + (1,))
    probs = jax.nn.softmax(jnp.concatenate([scores, sink_logit], axis=-1), axis=-1)[..., :-1]
    out = jnp.einsum('bgrnqk,bnkgd->bnqgrd', probs.astype(v.dtype), vv)
    return out.reshape(B, S, Q_W)


def causal_depthwise_conv(x, w, b):
    S = x.shape[1]
    xp = jnp.pad(x, ((0, 0), (CONV_WIDTH - 1, 0), (0, 0)))
    y = b + xp[:, 0:S] * w[0]
    for tap in range(1, CONV_WIDTH):
        y = y + xp[:, tap:tap + S] * w[tap]
    return y


def block_diag_linear(x, w, b):
    B, S = x.shape[0], x.shape[1]
    xb = x.reshape(B, S, LRU_BLOCKS, LRU_BLOCK_DIM)
    return jnp.einsum('bsnc,ncd->bsnd', xb, w).reshape(B, S, LRU_WIDTH) + b


def rg_lru(x, wa, ba, wx, bx, lam):
    r = jax.nn.sigmoid(block_diag_linear(x, wa, ba)).astype(jnp.float32)
    i = jax.nn.sigmoid(block_diag_linear(x, wx, bx))
    log_a = -LRU_C * r * jax.nn.softplus(-lam.astype(jnp.float32))
    a = jnp.exp(log_a)
    b = jnp.sqrt(-jnp.expm1(2.0 * log_a)) * (i * x).astype(jnp.float32)

    def combine(c1, c2):
        a1, b1 = c1
        a2, b2 = c2
        return a1 * a2, a2 * b1 + b2

    _, h = lax.associative_scan(combine, (a, b), axis=1)
    return h.astype(x.dtype)


def peer_ffn(x, wq, k1, k2, u, v):
    B, S, D = x.shape
    q = (x @ wq).astype(jnp.float32).reshape(B, S, PEER_HEADS, 2, PEER_HALF)
    s1 = jnp.einsum('bshc,kc->bshk', q[..., 0, :], k1.astype(jnp.float32))
    s2 = jnp.einsum('bshc,kc->bshk', q[..., 1, :], k2.astype(jnp.float32))
    v1, i1 = lax.top_k(s1, PEER_TOPK)
    v2, i2 = lax.top_k(s2, PEER_TOPK)
    cand = (v1[..., :, None] + v2[..., None, :]).reshape(B, S, PEER_HEADS, PEER_TOPK * PEER_TOPK)
    cand_idx = (i1[..., :, None] * PEER_KEYS + i2[..., None, :]).reshape(B, S, PEER_HEADS, PEER_TOPK * PEER_TOPK)
    top_s, pos = lax.top_k(cand, PEER_TOPK)
    idx = jnp.take_along_axis(cand_idx, pos, axis=-1)
    g = jax.nn.softmax(top_s, axis=-1).astype(x.dtype)
    n_chunks = (B * S) // PEER_CHUNK
    xc = x.reshape(n_chunks, PEER_CHUNK, D)
    ic = idx.reshape(n_chunks, PEER_CHUNK, PEER_HEADS, PEER_TOPK)
    gc = g.reshape(n_chunks, PEER_CHUNK, PEER_HEADS, PEER_TOPK)

    def expert_chunk(args):
        xt, it, gt = args
        ue = u[it]
        ve = v[it]
        act = jax.nn.gelu(jnp.einsum('chkd,cd->chk', ue, xt))
        return jnp.einsum('chk,chkd->cd', gt * act, ve)

    y = lax.map(expert_chunk, (xc, ic, gc))
    return y.reshape(B, S, D)


def setup_inputs(seed: int = 0) -> dict:
    key = jax.random.key(seed)
    ks = jax.random.split(key, 32)
    L, D = DEPTH, D_MODEL
    nrm = lambda k, shape, s: jax.random.normal(k, shape, jnp.float32) * s
    a0 = jax.random.uniform(ks[10], (L, LRU_WIDTH), jnp.float32, 0.9, 0.999)
    return {
        "x": nrm(ks[0], (BATCH, SEQ, D), 1.0),
        "p": nrm(ks[1], (L, BATCH, SEQ, PLE_DIM), 1.0),
        "norm_mix_g": 1.0 + nrm(ks[2], (L, D), 0.02),
        "w_in": nrm(ks[3], (L, D, IN_WIDTH), D ** -0.5),
        "attn_sink": nrm(ks[4], (L, N_Q_HEADS), 0.5),
        "conv_w": nrm(ks[5], (L, CONV_WIDTH, LRU_WIDTH), CONV_WIDTH ** -0.5),
        "conv_b": nrm(ks[6], (L, LRU_WIDTH), 0.02),
        "lru_wa": nrm(ks[7], (L, LRU_BLOCKS, LRU_BLOCK_DIM, LRU_BLOCK_DIM), LRU_BLOCK_DIM ** -0.5),
        "lru_ba": nrm(ks[8], (L, LRU_WIDTH), 0.02),
        "lru_wx": nrm(ks[9], (L, LRU_BLOCKS, LRU_BLOCK_DIM, LRU_BLOCK_DIM), LRU_BLOCK_DIM ** -0.5),
        "lru_bx": nrm(ks[11], (L, LRU_WIDTH), 0.02),
        "lru_lambda": jnp.log(a0) - jnp.log1p(-a0),
        "w_up_attn": nrm(ks[12], (L, Q_W, D), Q_W ** -0.5),
        "w_up_lru": nrm(ks[13], (L, LRU_WIDTH, D), LRU_WIDTH ** -0.5),
        "w_o": nrm(ks[14], (L, D, D), D ** -0.5),
        "norm_ffn_g": 1.0 + nrm(ks[15], (L, D), 0.02),
        "peer_wq": nrm(ks[16], (L, D, PEER_HEADS * PEER_QDIM), D ** -0.5),
        "peer_k1": nrm(ks[17], (L, PEER_KEYS, PEER_HALF), PEER_HALF ** -0.5),
        "peer_k2": nrm(ks[18], (L, PEER_KEYS, PEER_HALF), PEER_HALF ** -0.5),
        "peer_u": nrm(ks[19], (L, PEER_N_EXPERTS, D), D ** -0.5),
        "peer_v": nrm(ks[20], (L, PEER_N_EXPERTS, D), PEER_HEADS ** -0.5),
        "norm_ple_g": 1.0 + nrm(ks[21], (L, D), 0.02),
        "ple_w_gate": nrm(ks[22], (L, D, D), D ** -0.5),
        "ple_w_proj": nrm(ks[23], (L, PLE_DIM, D), PLE_DIM ** -0.5),
        "final_g": 1.0 + nrm(ks[24], (D,), 0.02),
    }


def reference(x, p, norm_mix_g, w_in, attn_sink, conv_w, conv_b, lru_wa, lru_ba, lru_wx, lru_bx,
              lru_lambda, w_up_attn, w_up_lru, w_o, norm_ffn_g, peer_wq, peer_k1, peer_k2, peer_u,
              peer_v, norm_ple_g, ple_w_gate, ple_w_proj, final_g):
    B, S, _ = x.shape
    h = x
    for l in range(DEPTH):
        xn = rms_norm(h, norm_mix_g[l])
        proj = xn @ w_in[l]
        q, k, v, lru_x, lru_gate, gate_attn, gate_lru = jnp.split(proj, SPLIT_POINTS, axis=-1)
        attn = sliding_window_attention(q.reshape(B, S, N_Q_HEADS, HEAD_DIM),
                                        k.reshape(B, S, N_KV_HEADS, HEAD_DIM),
                                        v.reshape(B, S, N_KV_HEADS, HEAD_DIM), attn_sink[l])
        lru_in = causal_depthwise_conv(lru_x, conv_w[l], conv_b[l])
        rec = rg_lru(lru_in, lru_wa[l], lru_ba[l], lru_wx[l], lru_bx[l], lru_lambda[l]) * jax.nn.gelu(lru_gate)
        merged = (jax.nn.sigmoid(gate_attn) * (attn @ w_up_attn[l])
                  + jax.nn.sigmoid(gate_lru) * (rec @ w_up_lru[l]))
        h = h + merged @ w_o[l]
        h = h + peer_ffn(rms_norm(h, norm_ffn_g[l]), peer_wq[l], peer_k1[l], peer_k2[l], peer_u[l], peer_v[l])
        gate = jax.nn.sigmoid(rms_norm(h, norm_ple_g[l]) @ ple_w_gate[l])
        h = h + gate * (p[l] @ ple_w_proj[l])
    return rms_norm(h, final_g)
```

```python
import functools

import jax
import jax.numpy as jnp
from jax import lax
from jax.experimental import pallas as pl
from jax.experimental.pallas import tpu as pltpu

D_MODEL = 1024
PLE_DIM = 256
EPS = 1e-6
NEG_INF = -1e30
N_Q_HEADS = 8
N_KV_HEADS = 2
HEAD_DIM = 64
WINDOW = 128
BLOCK = 128
LRU_BLOCKS = 8
LRU_BLOCK_DIM = 128
CONV_WIDTH = 4
LRU_C = 8.0
PEER_HEADS = 8
PEER_KEYS = 128
PEER_HALF = 128
PEER_TOPK = 16
Q_W = N_Q_HEADS * HEAD_DIM
KV_W = N_KV_HEADS * HEAD_DIM

V7X_LANES = 128
V7X_SUBLANES = 8
V7X_VMEM_LIMIT_BYTES = 56 * 1024 * 1024

F32 = jnp.float32
BF16 = jnp.bfloat16
I32 = jnp.int32

_NT = (((1,), (1,)), ((), ()))


def _gelu(x):
    return 0.5 * x * (1.0 + jnp.tanh(0.7978845608028654 * (x + 0.044715 * (x * x * x))))


def _sigmoid(x):
    return 1.0 / (1.0 + jnp.exp(-x))


def _rms_norm(xf, g):
    ms = jnp.mean(xf * xf, axis=-1, keepdims=True)
    return xf * lax.rsqrt(ms + EPS) * g


def _params(*sem):
    return pltpu.CompilerParams(dimension_semantics=sem, vmem_limit_bytes=V7X_VMEM_LIMIT_BYTES)


_IN_COLS = (Q_W, 2 * KV_W, 2 * KV_W, D_MODEL, D_MODEL, D_MODEL, D_MODEL)


def _in_proj_kernel(x_ref, g_ref, w_ref, *out_refs):
    xn = _rms_norm(x_ref[...], g_ref[...]).astype(BF16)
    c0 = 0
    for ref, width in zip(out_refs, _IN_COLS):
        ref[...] = jnp.dot(xn, w_ref[:, c0:c0 + width], preferred_element_type=F32).astype(ref.dtype)
        c0 += width


def _in_proj(x2, g, w, tm):
    t = x2.shape[0]
    n_in = sum(_IN_COLS)
    return pl.pallas_call(
        _in_proj_kernel,
        grid=(t // tm,),
        in_specs=[
            pl.BlockSpec((tm, D_MODEL), lambda i: (i, 0)),
            pl.BlockSpec((1, D_MODEL), lambda i: (0, 0)),
            pl.BlockSpec((D_MODEL, n_in), lambda i: (0, 0)),
        ],
        out_specs=[pl.BlockSpec((tm, c), lambda i: (i, 0)) for c in _IN_COLS],
        out_shape=[jax.ShapeDtypeStruct((t, c), BF16) for c in _IN_COLS],
        compiler_params=_params("parallel"),
        name="in_proj",
    )(x2, g, w)


def _attn_kernel(sink_ref, q_ref, kc_ref, kp_ref, vc_ref, vp_ref, o_ref):
    n = pl.program_id(1)
    qi = lax.broadcasted_iota(I32, (BLOCK, 2 * BLOCK), 0)
    kj = lax.broadcasted_iota(I32, (BLOCK, 2 * BLOCK), 1)
    dist_i = qi + BLOCK - kj
    dist = dist_i.astype(F32)
    valid = (dist_i >= 0) & (dist_i < WINDOW) & ((n * BLOCK + kj - BLOCK) >= 0)
    kk = jnp.concatenate([kp_ref[...], kc_ref[...]], axis=0)
    vv = jnp.concatenate([vp_ref[...], vc_ref[...]], axis=0)
    lane = lax.broadcasted_iota(I32, (BLOCK, V7X_LANES), 1)
    upper = lane >= HEAD_DIM
    scale = HEAD_DIM ** -0.5
    for pair in range(N_Q_HEADS // 2):
        grp = (2 * pair) // (N_Q_HEADS // N_KV_HEADS)
        qp = q_ref[:, pair * V7X_LANES:(pair + 1) * V7X_LANES]
        kk_g = kk[:, grp * V7X_LANES:(grp + 1) * V7X_LANES]
        vv_g = vv[:, grp * V7X_LANES:(grp + 1) * V7X_LANES]
        outs = []
        for sub in range(2):
            head = 2 * pair + sub
            keep = upper if sub else jnp.logical_not(upper)
            qm = jnp.where(keep, qp, jnp.zeros_like(qp))
            s = lax.dot_general(qm, kk_g, _NT, preferred_element_type=F32)
            slope = 2.0 ** (-8.0 * (head + 1) / N_Q_HEADS)
            s = jnp.where(valid, s * scale - slope * dist, NEG_INF)
            sk = sink_ref[head]
            m = jnp.maximum(jnp.max(s, axis=-1, keepdims=True), sk)
            e = jnp.exp(s - m)
            den = jnp.sum(e, axis=-1, keepdims=True) + jnp.exp(sk - m)
            p = (e / den).astype(BF16)
            outs.append(jnp.dot(p, vv_g, preferred_element_type=F32))
        o_pair = jnp.where(upper, outs[1], outs[0])
        o_ref[:, pair * V7X_LANES:(pair + 1) * V7X_LANES] = o_pair.astype(o_ref.dtype)


def _attention(sink, q3, k3, v3):
    b, s, _ = q3.shape
    nb = s // BLOCK
    cur = lambda bi, n: (bi, n, 0)
    prev = lambda bi, n: (bi, jnp.maximum(n - 1, 0), 0)
    return pl.pallas_call(
        _attn_kernel,
        grid=(b, nb),
        in_specs=[
            pl.BlockSpec(memory_space=pltpu.SMEM),
            pl.BlockSpec((None, BLOCK, Q_W), cur),
            pl.BlockSpec((None, BLOCK, 2 * KV_W), cur),
            pl.BlockSpec((None, BLOCK, 2 * KV_W), prev),
            pl.BlockSpec((None, BLOCK, 2 * KV_W), cur),
            pl.BlockSpec((None, BLOCK, 2 * KV_W), prev),
        ],
        out_specs=pl.BlockSpec((None, BLOCK, Q_W), cur),
        out_shape=jax.ShapeDtypeStruct((b, s, Q_W), BF16),
        compiler_params=_params("parallel", "parallel"),
        name="swa_attention",
    )(sink, q3, k3, k3, v3, v3)


def _lru_kernel(x_ref, gate_ref, cw_ref, cb_ref, wa_ref, ba_ref, wx_ref, bx_ref, lam_ref, o_ref):
    s_len = x_ref.shape[0]
    x = x_ref[...].astype(F32)
    row = lax.broadcasted_iota(I32, x.shape, 0)

    def shift(v, d, fill):
        return jnp.where(row >= d, pltpu.roll(v, d, axis=0), fill)

    cw = cw_ref[...]
    y = cb_ref[...] + shift(x, 3, 0.0) * cw[0:1]
    y = y + shift(x, 2, 0.0) * cw[1:2]
    y = y + shift(x, 1, 0.0) * cw[2:3]
    y = y + x * cw[3:4]
    yb = y.astype(BF16)
    r = _sigmoid(jnp.dot(yb, wa_ref[...], preferred_element_type=F32) + ba_ref[...])
    i = _sigmoid(jnp.dot(yb, wx_ref[...], preferred_element_type=F32) + bx_ref[...])
    z = -lam_ref[...]
    softplus = jnp.maximum(z, 0.0) + jnp.log1p(jnp.exp(-jnp.abs(z)))
    log_a = (-LRU_C) * r * softplus
    a = jnp.exp(log_a)
    b = jnp.sqrt(1.0 - a * a) * (i * y)
    d = 1
    while d < s_len:
        b = a * shift(b, d, 0.0) + b
        if 2 * d < s_len:
            a = a * shift(a, d, 1.0)
        d *= 2
    o_ref[...] = (b * _gelu(gate_ref[...].astype(F32))).astype(o_ref.dtype)


def _rg_lru(lx3, lg3, conv_w, conv_b, wa, ba, wx, bx, lam):
    b, s, c = lx3.shape
    bd = LRU_BLOCK_DIM
    act = lambda bi, ci: (bi, 0, ci)
    vec = lambda bi, ci: (0, ci)
    mat = lambda bi, ci: (ci, 0, 0)
    return pl.pallas_call(
        _lru_kernel,
        grid=(b, c // bd),
        in_specs=[
            pl.BlockSpec((None, s, bd), act),
            pl.BlockSpec((None, s, bd), act),
            pl.BlockSpec((CONV_WIDTH, bd), vec),
            pl.BlockSpec((1, bd), vec),
            pl.BlockSpec((None, bd, bd), mat),
            pl.BlockSpec((1, bd), vec),
            pl.BlockSpec((None, bd, bd), mat),
            pl.BlockSpec((1, bd), vec),
            pl.BlockSpec((1, bd), vec),
        ],
        out_specs=pl.BlockSpec((None, s, bd), act),
        out_shape=jax.ShapeDtypeStruct((b, s, c), BF16),
        compiler_params=_params("parallel", "parallel"),
        name="rg_lru",
    )(lx3, lg3, conv_w, conv_b, wa, ba, wx, bx, lam)


def _merge_kernel(attn_ref, rec_ref, ga_ref, gl_ref, x_ref, wua_ref, wul_ref, wo_ref, g2_ref, wq_ref,
                  h_ref, xn_ref, qp_ref):
    up_a = jnp.dot(attn_ref[...], wua_ref[...], preferred_element_type=F32)
    up_l = jnp.dot(rec_ref[...], wul_ref[...], preferred_element_type=F32)
    merged = _sigmoid(ga_ref[...].astype(F32)) * up_a + _sigmoid(gl_ref[...].astype(F32)) * up_l
    h = x_ref[...] + jnp.dot(merged.astype(BF16), wo_ref[...], preferred_element_type=F32)
    h_ref[...] = h
    xn = _rms_norm(h, g2_ref[...]).astype(BF16)
    xn_ref[...] = xn
    qp_ref[...] = jnp.dot(xn, wq_ref[...], preferred_element_type=F32).astype(qp_ref.dtype)


def _merge(attn, rec, ga, gl, x2, wua, wul, wo, g2, wq, tm):
    t = x2.shape[0]
    nq = wq.shape[1]
    tok = lambda w: pl.BlockSpec((tm, w), lambda i: (i, 0))
    full = lambda a: pl.BlockSpec(a.shape, lambda i: (0, 0))
    return pl.pallas_call(
        _merge_kernel,
        grid=(t // tm,),
        in_specs=[tok(Q_W), tok(D_MODEL), tok(D_MODEL), tok(D_MODEL), tok(D_MODEL),
                  full(wua), full(wul), full(wo), full(g2), full(wq)],
        out_specs=[tok(D_MODEL), tok(D_MODEL), tok(nq)],
        out_shape=[jax.ShapeDtypeStruct((t, D_MODEL), F32),
                   jax.ShapeDtypeStruct((t, D_MODEL), BF16),
                   jax.ShapeDtypeStruct((t, nq), BF16)],
        compiler_params=_params("parallel"),
        name="merge_proj",
    )(attn, rec, ga, gl, x2, wua, wul, wo, g2, wq)


def _top16_rows(s, v_scr, i_scr):
    n = s.shape[0]
    key = lax.broadcasted_iota(I32, s.shape, 0)
    for j in range(PEER_TOPK):
        m = jnp.max(s, axis=0, keepdims=True)
        idx = jnp.min(jnp.where(s == m, key, n), axis=0, keepdims=True)
        v_scr[j:j + 1, :] = m
        i_scr[j:j + 1, :] = idx
        s = jnp.where(key == idx, -jnp.inf, s)


def _topk_kernel(qp_ref, k1_ref, k2_ref, g_ref, a_ref, b_ref,
                 v1_scr, i1_scr, v2_scr, i2_scr, ts_scr, g_scr, a_scr, b_scr):
    tm = qp_ref.shape[0]
    half = PEER_TOPK // 2

    def head_body(h, carry):
        off = pl.multiple_of(h * (2 * PEER_HALF), 2 * PEER_HALF)
        q1 = qp_ref[:, pl.ds(off, PEER_HALF)]
        q2 = qp_ref[:, pl.ds(off + PEER_HALF, PEER_HALF)]
        s1 = lax.dot_general(k1_ref[...], q1, _NT, preferred_element_type=F32)
        s2 = lax.dot_general(k2_ref[...], q2, _NT, preferred_element_type=F32)
        _top16_rows(s1, v1_scr, i1_scr)
        _top16_rows(s2, v2_scr, i2_scr)
        v1 = v1_scr[...]
        i1 = i1_scr[...]
        v2 = v2_scr[...]
        i2 = i2_scr[...]
        l16 = lax.broadcasted_iota(I32, (PEER_TOPK, tm), 0)
        l8 = lax.broadcasted_iota(I32, (half, tm), 0)
        cs = [v1[0:1] + v2]
        ca = [jnp.broadcast_to(i1[0:1], (PEER_TOPK, tm))]
        cb = [i2]
        cf = [l16]
        for j in range(1, half):
            cs.append(v1[j:j + 1] + v2[0:half])
            ca.append(jnp.broadcast_to(i1[j:j + 1], (half, tm)))
            cb.append(i2[0:half])
            cf.append(l8 + j * PEER_TOPK)
        cs.append(v1[half:] + v2[0:1])
        ca.append(i1[half:])
        cb.append(jnp.broadcast_to(i2[0:1], (half, tm)))
        cf.append((l8 + half) * PEER_TOPK)
        c = jnp.concatenate(cs, axis=0)
        cand_a = jnp.concatenate(ca, axis=0)
        cand_b = jnp.concatenate(cb, axis=0)
        flat = jnp.concatenate(cf, axis=0)
        base = pl.multiple_of(h * PEER_TOPK, PEER_TOPK)
        for k in range(PEER_TOPK):
            m = jnp.max(c, axis=0, keepdims=True)
            pick = jnp.min(jnp.where(c == m, flat, PEER_TOPK * PEER_TOPK), axis=0, keepdims=True)
            hit = flat == pick
            ts_scr[k:k + 1, :] = m
            a_scr[pl.ds(base + k, 1), :] = jnp.max(jnp.where(hit, cand_a, -1), axis=0, keepdims=True)
            b_scr[pl.ds(base + k, 1), :] = jnp.max(jnp.where(hit, cand_b, -1), axis=0, keepdims=True)
            c = jnp.where(hit, -jnp.inf, c)
        ts = ts_scr[...]
        e = jnp.exp(ts - ts[0:1])
        g_scr[pl.ds(base, PEER_TOPK), :] = e / jnp.sum(e, axis=0, keepdims=True)
        return carry

    lax.fori_loop(0, PEER_HEADS, head_body, 0)
    g_ref[...] = g_scr[...].T
    a_ref[...] = a_scr[...].T
    b_ref[...] = b_scr[...].T


def _peer_topk(qp, k1, k2, tm):
    t = qp.shape[0]
    nsel = PEER_HEADS * PEER_TOPK
    tok = pl.BlockSpec((tm, nsel), lambda i: (i, 0))
    keys = pl.BlockSpec((PEER_KEYS, PEER_HALF), lambda i: (0, 0))
    return pl.pallas_call(
        _topk_kernel,
        grid=(t // tm,),
        in_specs=[pl.BlockSpec((tm, qp.shape[1]), lambda i: (i, 0)), keys, keys],
        out_specs=[tok, tok, tok],
        out_shape=[jax.ShapeDtypeStruct((t, nsel), F32),
                   jax.ShapeDtypeStruct((t, nsel), I32),
                   jax.ShapeDtypeStruct((t, nsel), I32)],
        scratch_shapes=[
            pltpu.VMEM((PEER_TOPK, tm), F32), pltpu.VMEM((PEER_TOPK, tm), I32),
            pltpu.VMEM((PEER_TOPK, tm), F32), pltpu.VMEM((PEER_TOPK, tm), I32),
            pltpu.VMEM((PEER_TOPK, tm), F32),
            pltpu.VMEM((nsel, tm), F32), pltpu.VMEM((nsel, tm), I32), pltpu.VMEM((nsel, tm), I32),
        ],
        compiler_params=_params("parallel"),
        name="peer_topk",
    )(qp, k1, k2)


def _wbuild_kernel(g_ref, a_ref, b_ref, o_ref):
    tm = g_ref.shape[0]
    key = lax.broadcasted_iota(I32, (PEER_KEYS, g_ref.shape[1]), 0)

    def body(t, carry):
        g = g_ref[pl.ds(t, 1), :]
        a = a_ref[pl.ds(t, 1), :]
        b = b_ref[pl.ds(t, 1), :]
        sel_a = jnp.where(key == a, g, 0.0).astype(BF16)
        sel_b = jnp.where(key == b, 1.0, 0.0).astype(BF16)
        w = lax.dot_general(sel_a, sel_b, _NT, preferred_element_type=F32)
        row0 = pl.multiple_of(t * V7X_SUBLANES, V7X_SUBLANES)
        o_ref[:, pl.ds(row0, V7X_SUBLANES), :] = w.reshape(
            PEER_KEYS // V7X_SUBLANES, V7X_SUBLANES, PEER_KEYS)
        return carry

    lax.fori_loop(0, tm, body, 0, unroll=2)


def _peer_wbuild(g, a, b, tm):
    t, nsel = g.shape
    tok = pl.BlockSpec((tm, nsel), lambda i: (i, 0))
    na = PEER_KEYS // V7X_SUBLANES
    return pl.pallas_call(
        _wbuild_kernel,
        grid=(t // tm,),
        in_specs=[tok, tok, tok],
        out_specs=pl.BlockSpec((na, tm * V7X_SUBLANES, PEER_KEYS), lambda i: (0, i, 0)),
        out_shape=jax.ShapeDtypeStruct((na, t * V7X_SUBLANES, PEER_KEYS), F32),
        compiler_params=_params("parallel"),
        name="peer_wbuild",
    )(g, a, b)


def _ffn_kernel(x_ref, u_ref, v_ref, w_ref, y_ref):
    tm = x_ref.shape[0]

    @pl.when(pl.program_id(1) == 0)
    def _():
        y_ref[...] = jnp.zeros_like(y_ref)

    g = lax.dot_general(x_ref[...], u_ref[...], _NT, preferred_element_type=F32)
    hs = []
    for al in range(V7X_SUBLANES):
        wa = w_ref[pl.ds(al, tm, stride=V7X_SUBLANES), :]
        hs.append((wa * _gelu(g[:, al * PEER_KEYS:(al + 1) * PEER_KEYS])).astype(BF16))
    h = jnp.concatenate(hs, axis=1)
    y_ref[...] += jnp.dot(h, v_ref[...], preferred_element_type=F32)


def _peer_ffn(xn, u, v, w, tm):
    t = xn.shape[0]
    ne = V7X_SUBLANES * PEER_KEYS
    return pl.pallas_call(
        _ffn_kernel,
        grid=(t // tm, u.shape[0] // ne),
        in_specs=[
            pl.BlockSpec((tm, D_MODEL), lambda i, n: (i, 0)),
            pl.BlockSpec((ne, D_MODEL), lambda i, n: (n, 0)),
            pl.BlockSpec((ne, D_MODEL), lambda i, n: (n, 0)),
            pl.BlockSpec((None, tm * V7X_SUBLANES, PEER_KEYS), lambda i, n: (n, i, 0)),
        ],
        out_specs=pl.BlockSpec((tm, D_MODEL), lambda i, n: (i, 0)),
        out_shape=jax.ShapeDtypeStruct((t, D_MODEL), F32),
        compiler_params=_params("parallel", "arbitrary"),
        name="peer_ffn",
    )(xn, u, v, w)


def _final_kernel(h_ref, y_ref, p_ref, g3_ref, wg_ref, wp_ref, gf_ref, o_ref):
    h = h_ref[...] + y_ref[...]
    xn = _rms_norm(h, g3_ref[...]).astype(BF16)
    gate = _sigmoid(jnp.dot(xn, wg_ref[...], preferred_element_type=F32))
    proj = jnp.dot(p_ref[...].astype(BF16), wp_ref[...], preferred_element_type=F32)
    o_ref[...] = _rms_norm(h + gate * proj, gf_ref[...])


def _final(h1, y, p2, g3, wg, wp, gf, tm):
    t = h1.shape[0]
    tok = lambda w: pl.BlockSpec((tm, w), lambda i: (i, 0))
    full = lambda a: pl.BlockSpec(a.shape, lambda i: (0, 0))
    return pl.pallas_call(
        _final_kernel,
        grid=(t // tm,),
        in_specs=[tok(D_MODEL), tok(D_MODEL), tok(PLE_DIM), full(g3), full(wg), full(wp), full(gf)],
        out_specs=tok(D_MODEL),
        out_shape=jax.ShapeDtypeStruct((t, D_MODEL), F32),
        compiler_params=_params("parallel"),
        name="ple_final",
    )(h1, y, p2, g3, wg, wp, gf)


_TM_PROJ = 512
_TM_TOPK = 256
_TM_WBUILD = 128
_TM_FFN = 512


def _double_heads(w):
    d = w.shape[0]
    w = w.reshape(d, N_KV_HEADS, 1, HEAD_DIM)
    return jnp.broadcast_to(w, (d, N_KV_HEADS, 2, HEAD_DIM)).reshape(d, 2 * KV_W)


def kernel(x, p, norm_mix_g, w_in, attn_sink, conv_w, conv_b, lru_wa, lru_ba, lru_wx, lru_bx, lru_lambda,
           w_up_attn, w_up_lru, w_o, norm_ffn_g, peer_wq, peer_k1, peer_k2, peer_u, peer_v, norm_ple_g,
           ple_w_gate, ple_w_proj, final_g):
    bsz, seq, d = x.shape
    t = bsz * seq
    assert p.shape[0] == 1, "single-layer block: the final norm follows layer 0"
    h = x.reshape(t, d)
    for l in range(1):
        row = lambda a: a.reshape(1, -1)
        wi = w_in[l]
        w_comb = jnp.concatenate(
            [wi[:, :Q_W], _double_heads(wi[:, Q_W:Q_W + KV_W]), _double_heads(wi[:, Q_W + KV_W:Q_W + 2 * KV_W]),
             wi[:, Q_W + 2 * KV_W:]], axis=1).astype(BF16)
        q, kd, vd, lx, lg, ga, gl = _in_proj(h, row(norm_mix_g[l]), w_comb, _TM_PROJ)
        attn = _attention(attn_sink[l], q.reshape(bsz, seq, -1), kd.reshape(bsz, seq, -1),
                          vd.reshape(bsz, seq, -1)).reshape(t, Q_W)
        rec = _rg_lru(lx.reshape(bsz, seq, d), lg.reshape(bsz, seq, d), conv_w[l], row(conv_b[l]),
                      lru_wa[l].astype(BF16), row(lru_ba[l]), lru_wx[l].astype(BF16), row(lru_bx[l]),
                      row(lru_lambda[l])).reshape(t, d)
        h1, xn2, qp = _merge(attn, rec, ga, gl, h, w_up_attn[l].astype(BF16), w_up_lru[l].astype(BF16),
                             w_o[l].astype(BF16), row(norm_ffn_g[l]), peer_wq[l].astype(BF16), _TM_PROJ)
        gates, ia, ib = _peer_topk(qp, peer_k1[l].astype(BF16), peer_k2[l].astype(BF16), _TM_TOPK)
        w = _peer_wbuild(gates, ia, ib, _TM_WBUILD)
        y = _peer_ffn(xn2, peer_u[l].astype(BF16), peer_v[l].astype(BF16), w, _TM_FFN)
        h = _final(h1, y, p[l].reshape(t, PLE_DIM), row(norm_ple_g[l]), ple_w_gate[l].astype(BF16),
                   ple_w_proj[l].astype(BF16), row(final_g), _TM_PROJ)
    return h.reshape(bsz, seq, d)
```

```python
import functools

import jax
import jax.numpy as jnp
from jax import lax
from jax.experimental import pallas as pl
from jax.experimental.pallas import tpu as pltpu

D_MODEL = 1024
PLE_DIM = 256
EPS = 1e-6
NEG_INF = -1e30
N_Q_HEADS = 8
N_KV_HEADS = 2
HEAD_DIM = 64
WINDOW = 128
BLOCK = 128
LRU_BLOCKS = 8
LRU_BLOCK_DIM = 128
CONV_WIDTH = 4
LRU_C = 8.0
PEER_HEADS = 8
PEER_KEYS = 128
PEER_HALF = 128
PEER_TOPK = 16
Q_W = N_Q_HEADS * HEAD_DIM
KV_W = N_KV_HEADS * HEAD_DIM

V7X_LANES = 128
V7X_SUBLANES = 8
V7X_VMEM_LIMIT_BYTES = 56 * 1024 * 1024

F32 = jnp.float32
BF16 = jnp.bfloat16
I32 = jnp.int32

_NT = (((1,), (1,)), ((), ()))


def _gelu(x):
    return 0.5 * x * (1.0 + jnp.tanh(0.7978845608028654 * (x + 0.044715 * (x * x * x))))


def _sigmoid(x):
    return 1.0 / (1.0 + jnp.exp(-x))


def _rms_norm(xf, g):
    ms = jnp.mean(xf * xf, axis=-1, keepdims=True)
    return xf * lax.rsqrt(ms + EPS) * g


def _params(*sem):
    return pltpu.CompilerParams(dimension_semantics=sem, vmem_limit_bytes=V7X_VMEM_LIMIT_BYTES)


_IN_COLS = (Q_W, 2 * KV_W, 2 * KV_W, D_MODEL, D_MODEL, D_MODEL, D_MODEL)


def _in_proj_kernel(x_ref, g_ref, w_ref, *out_refs):
    xn = _rms_norm(x_ref[...], g_ref[...]).astype(BF16)
    c0 = 0
    for ref, width in zip(out_refs, _IN_COLS):
        ref[...] = jnp.dot(xn, w_ref[:, c0:c0 + width], preferred_element_type=F32).astype(ref.dtype)
        c0 += width


def _in_proj(x2, g, w, tm):
    t = x2.shape[0]
    n_in = sum(_IN_COLS)
    return pl.pallas_call(
        _in_proj_kernel,
        grid=(t // tm,),
        in_specs=[
            pl.BlockSpec((tm, D_MODEL), lambda i: (i, 0)),
            pl.BlockSpec((1, D_MODEL), lambda i: (0, 0)),
            pl.BlockSpec((D_MODEL, n_in), lambda i: (0, 0)),
        ],
        out_specs=[pl.BlockSpec((tm, c), lambda i: (i, 0)) for c in _IN_COLS],
        out_shape=[jax.ShapeDtypeStruct((t, c), BF16) for c in _IN_COLS],
        compiler_params=_params("parallel"),
        name="in_proj",
    )(x2, g, w)


def _attn_kernel(sink_ref, q_ref, kc_ref, kp_ref, vc_ref, vp_ref, o_ref):
    n = pl.program_id(1)
    qi = lax.broadcasted_iota(I32, (BLOCK, 2 * BLOCK), 0)
    kj = lax.broadcasted_iota(I32, (BLOCK, 2 * BLOCK), 1)
    dist_i = qi + BLOCK - kj
    dist = dist_i.astype(F32)
    valid = (dist_i >= 0) & (dist_i < WINDOW) & ((n * BLOCK + kj - BLOCK) >= 0)
    kk = jnp.concatenate([kp_ref[...], kc_ref[...]], axis=0)
    vv = jnp.concatenate([vp_ref[...], vc_ref[...]], axis=0)
    lane = lax.broadcasted_iota(I32, (BLOCK, V7X_LANES), 1)
    upper = lane >= HEAD_DIM
    scale = HEAD_DIM ** -0.5
    for pair in range(N_Q_HEADS // 2):
        grp = (2 * pair) // (N_Q_HEADS // N_KV_HEADS)
        qp = q_ref[:, pair * V7X_LANES:(pair + 1) * V7X_LANES]
        kk_g = kk[:, grp * V7X_LANES:(grp + 1) * V7X_LANES]
        vv_g = vv[:, grp * V7X_LANES:(grp + 1) * V7X_LANES]
        outs = []
        for sub in range(2):
            head = 2 * pair + sub
            keep = upper if sub else jnp.logical_not(upper)
            qm = jnp.where(keep, qp, jnp.zeros_like(qp))
            s = lax.dot_general(qm, kk_g, _NT, preferred_element_type=F32)
            slope = 2.0 ** (-8.0 * (head + 1) / N_Q_HEADS)
            s = jnp.where(valid, s * scale - slope * dist, NEG_INF)
            sk = sink_ref[head]
            m = jnp.maximum(jnp.max(s, axis=-1, keepdims=True), sk)
            e = jnp.exp(s - m)
            den = jnp.sum(e, axis=-1, keepdims=True) + jnp.exp(sk - m)
            p = (e / den).astype(BF16)
            outs.append(jnp.dot(p, vv_g, preferred_element_type=F32))
        o_pair = jnp.where(upper, outs[1], outs[0])
        o_ref[:, pair * V7X_LANES:(pair + 1) * V7X_LANES] = o_pair.astype(o_ref.dtype)


def _attention(sink, q3, k3, v3):
    b, s, _ = q3.shape
    nb = s // BLOCK
    cur = lambda bi, n: (bi, n, 0)
    prev = lambda bi, n: (bi, jnp.maximum(n - 1, 0), 0)
    return pl.pallas_call(
        _attn_kernel,
        grid=(b, nb),
        in_specs=[
            pl.BlockSpec(memory_space=pltpu.SMEM),
            pl.BlockSpec((None, BLOCK, Q_W), cur),
            pl.BlockSpec((None, BLOCK, 2 * KV_W), cur),
            pl.BlockSpec((None, BLOCK, 2 * KV_W), prev),
            pl.BlockSpec((None, BLOCK, 2 * KV_W), cur),
            pl.BlockSpec((None, BLOCK, 2 * KV_W), prev),
        ],
        out_specs=pl.BlockSpec((None, BLOCK, Q_W), cur),
        out_shape=jax.ShapeDtypeStruct((b, s, Q_W), BF16),
        compiler_params=_params("parallel", "parallel"),
        name="swa_attention",
    )(sink, q3, k3, k3, v3, v3)


def _lru_kernel(x_ref, gate_ref, cw_ref, cb_ref, wa_ref, ba_ref, wx_ref, bx_ref, lam_ref, o_ref):
    s_len = x_ref.shape[0]
    x = x_ref[...].astype(F32)
    row = lax.broadcasted_iota(I32, x.shape, 0)

    def shift(v, d, fill):
        return jnp.where(row >= d, pltpu.roll(v, d, axis=0), fill)

    cw = cw_ref[...]
    y = cb_ref[...] + shift(x, 3, 0.0) * cw[0:1]
    y = y + shift(x, 2, 0.0) * cw[1:2]
    y = y + shift(x, 1, 0.0) * cw[2:3]
    y = y + x * cw[3:4]
    yb = y.astype(BF16)
    r = _sigmoid(jnp.dot(yb, wa_ref[...], preferred_element_type=F32) + ba_ref[...])
    i = _sigmoid(jnp.dot(yb, wx_ref[...], preferred_element_type=F32) + bx_ref[...])
    z = -lam_ref[...]
    softplus = jnp.maximum(z, 0.0) + jnp.log1p(jnp.exp(-jnp.abs(z)))
    log_a = (-LRU_C) * r * softplus
    a = jnp.exp(log_a)
    b = jnp.sqrt(1.0 - a * a) * (i * y)
    d = 1
    while d < s_len:
        b = a * shift(b, d, 0.0) + b
        if 2 * d < s_len:
            a = a * shift(a, d, 1.0)
        d *= 2
    o_ref[...] = (b * _gelu(gate_ref[...].astype(F32))).astype(o_ref.dtype)


def _rg_lru(lx3, lg3, conv_w, conv_b, wa, ba, wx, bx, lam):
    b, s, c = lx3.shape
    bd = LRU_BLOCK_DIM
    act = lambda bi, ci: (bi, 0, ci)
    vec = lambda bi, ci: (0, ci)
    mat = lambda bi, ci: (ci, 0, 0)
    return pl.pallas_call(
        _lru_kernel,
        grid=(b, c // bd),
        in_specs=[
            pl.BlockSpec((None, s, bd), act),
            pl.BlockSpec((None, s, bd), act),
            pl.BlockSpec((CONV_WIDTH, bd), vec),
            pl.BlockSpec((1, bd), vec),
            pl.BlockSpec((None, bd, bd), mat),
            pl.BlockSpec((1, bd), vec),
            pl.BlockSpec((None, bd, bd), mat),
            pl.BlockSpec((1, bd), vec),
            pl.BlockSpec((1, bd), vec),
        ],
        out_specs=pl.BlockSpec((None, s, bd), act),
        out_shape=jax.ShapeDtypeStruct((b, s, c), BF16),
        compiler_params=_params("parallel", "parallel"),
        name="rg_lru",
    )(lx3, lg3, conv_w, conv_b, wa, ba, wx, bx, lam)


def _merge_kernel(attn_ref, rec_ref, ga_ref, gl_ref, x_ref, wua_ref, wul_ref, wo_ref, g2_ref, wq_ref,
                  h_ref, xn_ref, qp_ref):
    up_a = jnp.dot(attn_ref[...], wua_ref[...], preferred_element_type=F32)
    up_l = jnp.dot(rec_ref[...], wul_ref[...], preferred_element_type=F32)
    merged = _sigmoid(ga_ref[...].astype(F32)) * up_a + _sigmoid(gl_ref[...].astype(F32)) * up_l
    h = x_ref[...] + jnp.dot(merged.astype(BF16), wo_ref[...], preferred_element_type=F32)
    h_ref[...] = h
    xn = _rms_norm(h, g2_ref[...]).astype(BF16)
    xn_ref[...] = xn
    qp_ref[...] = jnp.dot(xn, wq_ref[...], preferred_element_type=F32).astype(qp_ref.dtype)


def _merge(attn, rec, ga, gl, x2, wua, wul, wo, g2, wq, tm):
    t = x2.shape[0]
    nq = wq.shape[1]
    tok = lambda w: pl.BlockSpec((tm, w), lambda i: (i, 0))
    full = lambda a: pl.BlockSpec(a.shape, lambda i: (0, 0))
    return pl.pallas_call(
        _merge_kernel,
        grid=(t // tm,),
        in_specs=[tok(Q_W), tok(D_MODEL), tok(D_MODEL), tok(D_MODEL), tok(D_MODEL),
                  full(wua), full(wul), full(wo), full(g2), full(wq)],
        out_specs=[tok(D_MODEL), tok(D_MODEL), tok(nq)],
        out_shape=[jax.ShapeDtypeStruct((t, D_MODEL), F32),
                   jax.ShapeDtypeStruct((t, D_MODEL), BF16),
                   jax.ShapeDtypeStruct((t, nq), BF16)],
        compiler_params=_params("parallel"),
        name="merge_proj",
    )(attn, rec, ga, gl, x2, wua, wul, wo, g2, wq)


def _oddeven_merge(lo, hi, r):
    step = r * 2
    if step < hi - lo:
        yield from _oddeven_merge(lo, hi, step)
        yield from _oddeven_merge(lo + r, hi, step)
        yield from [(i, i + r) for i in range(lo + r, hi - r, step)]
    else:
        yield (lo, lo + r)


def _oddeven_merge_sort(lo, hi):
    if hi - lo >= 1:
        mid = lo + (hi - lo) // 2
        yield from _oddeven_merge_sort(lo, mid)
        yield from _oddeven_merge_sort(mid + 1, hi)
        yield from _oddeven_merge(lo, hi, 1)


_SORT16 = tuple(_oddeven_merge_sort(0, PEER_TOPK - 1))


def _precedes(a, b):
    va, pa = a
    vb, pb = b
    if isinstance(pa, int) and isinstance(pb, int):
        return (va >= vb) if pa < pb else (va > vb)
    return (va > vb) | ((va == vb) & (pa < pb))


def _first(a, b):
    if b is None:
        return a
    if a is None:
        return b
    return jnp.maximum(a[0], b[0]), jnp.where(_precedes(a, b), a[1], b[1])


def _exchange(a, b):
    if b is None:
        return a, None
    if a is None:
        return b, None
    c = _precedes(a, b)
    return ((jnp.maximum(a[0], b[0]), jnp.where(c, a[1], b[1])),
            (jnp.minimum(a[0], b[0]), jnp.where(c, b[1], a[1])))


def _sort16(items):
    items = list(items)
    for i, j in _SORT16:
        items[i], items[j] = _exchange(items[i], items[j])
    return items


def _merge16(xs, ys):
    n = PEER_TOPK
    xs = list(xs) + [None] * (n - len(xs))
    ys = list(ys) + [None] * (n - len(ys))
    h = [_first(xs[i], ys[n - 1 - i]) for i in range(n)]
    d = n // 2
    while d:
        for i in range(n):
            if not i & d:
                h[i], h[i + d] = _exchange(h[i], h[i + d])
        d //= 2
    return [x for x in h if x is not None]


def _top16_of_keys(s_ref):
    runs = []
    for g in range(PEER_KEYS // PEER_TOPK):
        keys = range(g * PEER_TOPK, (g + 1) * PEER_TOPK)
        runs.append(_sort16([(s_ref[pl.ds(k, PEER_HEADS, stride=PEER_KEYS), :], k) for k in keys]))
    while len(runs) > 1:
        runs = [_merge16(runs[i], runs[i + 1]) for i in range(0, len(runs), 2)]
    return runs[0]


_EXPERT_BITS = 14


def _topk_kernel(qp_ref, k1_ref, k2_ref, g_ref, e_ref, s1_scr, s2_scr):
    for h in range(PEER_HEADS):
        off = h * 2 * PEER_HALF
        rows = pl.ds(h * PEER_KEYS, PEER_KEYS)
        s1_scr[rows, :] = lax.dot_general(k1_ref[...], qp_ref[:, off:off + PEER_HALF], _NT,
                                          preferred_element_type=F32)
        s2_scr[rows, :] = lax.dot_general(k2_ref[...], qp_ref[:, off + PEER_HALF:off + 2 * PEER_HALF], _NT,
                                          preferred_element_type=F32)
    top1 = _top16_of_keys(s1_scr)
    top2 = _top16_of_keys(s2_scr)

    def pair(j, l):
        expert = top1[j][1] * PEER_KEYS + top2[l][1]
        return top1[j][0] + top2[l][0], expert + ((j * PEER_TOPK + l) << _EXPERT_BITS)

    half = PEER_TOPK // 2
    rows = [[pair(j, l) for l in range(PEER_TOPK // (j + 1))] for j in range(half)]
    col = [pair(j, 0) for j in range(half, PEER_TOPK)]
    small = _merge16(_merge16(rows[2], rows[3]),
                     _merge16(_merge16(rows[4], rows[5]), _merge16(rows[6], rows[7])))
    top = _merge16(rows[0], _merge16(_merge16(rows[1], col), small))
    ex = [jnp.exp(v - top[0][0]) for v, _ in top]
    den = ex[0]
    for x in ex[1:]:
        den = den + x
    gates = jnp.concatenate([x / den for x in ex], axis=0)
    experts = jnp.concatenate([p & ((1 << _EXPERT_BITS) - 1) for _, p in top], axis=0)
    g_ref[...] = gates.T
    e_ref[...] = experts.T


def _peer_topk(qp, k1, k2, tm):
    t = qp.shape[0]
    nsel = PEER_HEADS * PEER_TOPK
    tok = pl.BlockSpec((tm, nsel), lambda i: (i, 0))
    keys = pl.BlockSpec((PEER_KEYS, PEER_HALF), lambda i: (0, 0))
    return pl.pallas_call(
        _topk_kernel,
        grid=(t // tm,),
        in_specs=[pl.BlockSpec((tm, qp.shape[1]), lambda i: (i, 0)), keys, keys],
        out_specs=[tok, tok],
        out_shape=[jax.ShapeDtypeStruct((t, nsel), F32), jax.ShapeDtypeStruct((t, nsel), I32)],
        scratch_shapes=[pltpu.VMEM((PEER_HEADS * PEER_KEYS, tm), F32),
                        pltpu.VMEM((PEER_HEADS * PEER_KEYS, tm), F32)],
        compiler_params=_params("parallel"),
        name="peer_topk",
    )(qp, k1, k2)


def _wbuild_kernel(g_ref, e_ref, o_ref):
    tm = g_ref.shape[0]
    key = lax.broadcasted_iota(I32, (PEER_KEYS, g_ref.shape[1]), 0)

    def body(t, carry):
        g = g_ref[pl.ds(t, 1), :]
        e = e_ref[pl.ds(t, 1), :]
        a = e >> 7
        b = e & (PEER_KEYS - 1)
        sel_a = jnp.where(key == a, g, 0.0).astype(BF16)
        sel_b = jnp.where(key == b, 1.0, 0.0).astype(BF16)
        w = lax.dot_general(sel_a, sel_b, _NT, preferred_element_type=F32)
        row0 = pl.multiple_of(t * V7X_SUBLANES, V7X_SUBLANES)
        o_ref[:, pl.ds(row0, V7X_SUBLANES), :] = w.reshape(
            PEER_KEYS // V7X_SUBLANES, V7X_SUBLANES, PEER_KEYS)
        return carry

    lax.fori_loop(0, tm, body, 0, unroll=32)


def _peer_wbuild(g, e, tm):
    t, nsel = g.shape
    tok = pl.BlockSpec((tm, nsel), lambda i: (i, 0))
    na = PEER_KEYS // V7X_SUBLANES
    return pl.pallas_call(
        _wbuild_kernel,
        grid=(t // tm,),
        in_specs=[tok, tok],
        out_specs=pl.BlockSpec((na, tm * V7X_SUBLANES, PEER_KEYS), lambda i: (0, i, 0)),
        out_shape=jax.ShapeDtypeStruct((na, t * V7X_SUBLANES, PEER_KEYS), F32),
        compiler_params=_params("parallel"),
        name="peer_wbuild",
    )(g, e)


def _ffn_kernel(x_ref, u_ref, v_ref, w_ref, y_ref):
    tm = x_ref.shape[0]

    @pl.when(pl.program_id(1) == 0)
    def _():
        y_ref[...] = jnp.zeros_like(y_ref)

    g = lax.dot_general(x_ref[...], u_ref[...], _NT, preferred_element_type=F32)
    hs = []
    for al in range(V7X_SUBLANES):
        wa = w_ref[pl.ds(al, tm, stride=V7X_SUBLANES), :]
        hs.append((wa * _gelu(g[:, al * PEER_KEYS:(al + 1) * PEER_KEYS])).astype(BF16))
    h = jnp.concatenate(hs, axis=1)
    y_ref[...] += jnp.dot(h, v_ref[...], preferred_element_type=F32)


def _peer_ffn(xn, u, v, w, tm):
    t = xn.shape[0]
    ne = V7X_SUBLANES * PEER_KEYS
    return pl.pallas_call(
        _ffn_kernel,
        grid=(t // tm, u.shape[0] // ne),
        in_specs=[
            pl.BlockSpec((tm, D_MODEL), lambda i, n: (i, 0)),
            pl.BlockSpec((ne, D_MODEL), lambda i, n: (n, 0)),
            pl.BlockSpec((ne, D_MODEL), lambda i, n: (n, 0)),
            pl.BlockSpec((None, tm * V7X_SUBLANES, PEER_KEYS), lambda i, n: (n, i, 0)),
        ],
        out_specs=pl.BlockSpec((tm, D_MODEL), lambda i, n: (i, 0)),
        out_shape=jax.ShapeDtypeStruct((t, D_MODEL), F32),
        compiler_params=_params("parallel", "arbitrary"),
        name="peer_ffn",
    )(xn, u, v, w)


def _final_kernel(h_ref, y_ref, p_ref, g3_ref, wg_ref, wp_ref, gf_ref, o_ref):
    h = h_ref[...] + y_ref[...]
    xn = _rms_norm(h, g3_ref[...]).astype(BF16)
    gate = _sigmoid(jnp.dot(xn, wg_ref[...], preferred_element_type=F32))
    proj = jnp.dot(p_ref[...].astype(BF16), wp_ref[...], preferred_element_type=F32)
    o_ref[...] = _rms_norm(h + gate * proj, gf_ref[...])


def _final(h1, y, p2, g3, wg, wp, gf, tm):
    t = h1.shape[0]
    tok = lambda w: pl.BlockSpec((tm, w), lambda i: (i, 0))
    full = lambda a: pl.BlockSpec(a.shape, lambda i: (0, 0))
    return pl.pallas_call(
        _final_kernel,
        grid=(t // tm,),
        in_specs=[tok(D_MODEL), tok(D_MODEL), tok(PLE_DIM), full(g3), full(wg), full(wp), full(gf)],
        out_specs=tok(D_MODEL),
        out_shape=jax.ShapeDtypeStruct((t, D_MODEL), F32),
        compiler_params=_params("parallel"),
        name="ple_final",
    )(h1, y, p2, g3, wg, wp, gf)


_TM_PROJ = 512
_TM_TOPK = 128
_TM_WBUILD = 128
_TM_FFN = 512


def _double_heads(w):
    d = w.shape[0]
    w = w.reshape(d, N_KV_HEADS, 1, HEAD_DIM)
    return jnp.broadcast_to(w, (d, N_KV_HEADS, 2, HEAD_DIM)).reshape(d, 2 * KV_W)


def kernel(x, p, norm_mix_g, w_in, attn_sink, conv_w, conv_b, lru_wa, lru_ba, lru_wx, lru_bx, lru_lambda,
           w_up_attn, w_up_lru, w_o, norm_ffn_g, peer_wq, peer_k1, peer_k2, peer_u, peer_v, norm_ple_g,
           ple_w_gate, ple_w_proj, final_g):
    bsz, seq, d = x.shape
    t = bsz * seq
    assert p.shape[0] == 1, "single-layer block: the final norm follows layer 0"
    h = x.reshape(t, d)
    for l in range(1):
        row = lambda a: a.reshape(1, -1)
        wi = w_in[l]
        w_comb = jnp.concatenate(
            [wi[:, :Q_W], _double_heads(wi[:, Q_W:Q_W + KV_W]), _double_heads(wi[:, Q_W + KV_W:Q_W + 2 * KV_W]),
             wi[:, Q_W + 2 * KV_W:]], axis=1).astype(BF16)
        q, kd, vd, lx, lg, ga, gl = _in_proj(h, row(norm_mix_g[l]), w_comb, _TM_PROJ)
        attn = _attention(attn_sink[l], q.reshape(bsz, seq, -1), kd.reshape(bsz, seq, -1),
                          vd.reshape(bsz, seq, -1)).reshape(t, Q_W)
        rec = _rg_lru(lx.reshape(bsz, seq, d), lg.reshape(bsz, seq, d), conv_w[l], row(conv_b[l]),
                      lru_wa[l].astype(BF16), row(lru_ba[l]), lru_wx[l].astype(BF16), row(lru_bx[l]),
                      row(lru_lambda[l])).reshape(t, d)
        h1, xn2, qp = _merge(attn, rec, ga, gl, h, w_up_attn[l].astype(BF16), w_up_lru[l].astype(BF16),
                             w_o[l].astype(BF16), row(norm_ffn_g[l]), peer_wq[l].astype(BF16), _TM_PROJ)
        gates, experts = _peer_topk(qp, peer_k1[l].astype(BF16), peer_k2[l].astype(BF16), _TM_TOPK)
        w = _peer_wbuild(gates, experts, _TM_WBUILD)
        y = _peer_ffn(xn2, peer_u[l].astype(BF16), peer_v[l].astype(BF16), w, _TM_FFN)
        h = _final(h1, y, p[l].reshape(t, PLE_DIM), row(norm_ple_g[l]), ple_w_gate[l].astype(BF16),
                   ple_w_proj[l].astype(BF16), row(final_g), _TM_PROJ)
    return h.reshape(bsz, seq, d)
```

```python
import functools

import jax
import jax.numpy as jnp
from jax import lax
from jax.experimental import pallas as pl
from jax.experimental.pallas import tpu as pltpu

D_MODEL = 1024
PLE_DIM = 256
EPS = 1e-6
NEG_INF = -1e30
N_Q_HEADS = 8
N_KV_HEADS = 2
HEAD_DIM = 64
WINDOW = 128
BLOCK = 128
LRU_BLOCKS = 8
LRU_BLOCK_DIM = 128
CONV_WIDTH = 4
LRU_C = 8.0
PEER_HEADS = 8
PEER_KEYS = 128
PEER_HALF = 128
PEER_TOPK = 16
Q_W = N_Q_HEADS * HEAD_DIM
KV_W = N_KV_HEADS * HEAD_DIM

V7X_LANES = 128
V7X_SUBLANES = 8
V7X_VMEM_LIMIT_BYTES = 56 * 1024 * 1024

F32 = jnp.float32
BF16 = jnp.bfloat16
I32 = jnp.int32

_NT = (((1,), (1,)), ((), ()))


def _gelu(x):
    return 0.5 * x * (1.0 + jnp.tanh(0.7978845608028654 * (x + 0.044715 * (x * x * x))))


def _sigmoid(x):
    return 0.5 * (jnp.tanh(0.5 * x) + 1.0)


def _rms_norm(xf, g):
    ms = jnp.mean(xf * xf, axis=-1, keepdims=True)
    return xf * lax.rsqrt(ms + EPS) * g


def _params(*sem):
    return pltpu.CompilerParams(dimension_semantics=sem, vmem_limit_bytes=V7X_VMEM_LIMIT_BYTES)


_IN_COLS = (Q_W, 2 * KV_W, 2 * KV_W, D_MODEL, D_MODEL, D_MODEL, D_MODEL)


def _in_proj_kernel(x_ref, g_ref, w_ref, *out_refs):
    xn = _rms_norm(x_ref[...], g_ref[...]).astype(BF16)
    c0 = 0
    for ref, width in zip(out_refs, _IN_COLS):
        ref[...] = jnp.dot(xn, w_ref[:, c0:c0 + width], preferred_element_type=F32).astype(ref.dtype)
        c0 += width


def _in_proj(x2, g, w, tm):
    t = x2.shape[0]
    n_in = sum(_IN_COLS)
    return pl.pallas_call(
        _in_proj_kernel,
        grid=(t // tm,),
        in_specs=[
            pl.BlockSpec((tm, D_MODEL), lambda i: (i, 0)),
            pl.BlockSpec((1, D_MODEL), lambda i: (0, 0)),
            pl.BlockSpec((D_MODEL, n_in), lambda i: (0, 0)),
        ],
        out_specs=[pl.BlockSpec((tm, c), lambda i: (i, 0)) for c in _IN_COLS],
        out_shape=[jax.ShapeDtypeStruct((t, c), BF16) for c in _IN_COLS],
        compiler_params=_params("parallel"),
        name="in_proj",
    )(x2, g, w)


def _attn_kernel(sink_ref, q_ref, kc_ref, kp_ref, vc_ref, vp_ref, o_ref):
    n = pl.program_id(1)
    qi = lax.broadcasted_iota(I32, (BLOCK, 2 * BLOCK), 0)
    kj = lax.broadcasted_iota(I32, (BLOCK, 2 * BLOCK), 1)
    dist_i = qi + BLOCK - kj
    dist = dist_i.astype(F32)
    valid = (dist_i >= 0) & (dist_i < WINDOW) & ((n * BLOCK + kj - BLOCK) >= 0)
    kk = jnp.concatenate([kp_ref[...], kc_ref[...]], axis=0)
    vv = jnp.concatenate([vp_ref[...], vc_ref[...]], axis=0)
    lane = lax.broadcasted_iota(I32, (BLOCK, V7X_LANES), 1)
    upper = lane >= HEAD_DIM
    scale = HEAD_DIM ** -0.5
    for pair in range(N_Q_HEADS // 2):
        grp = (2 * pair) // (N_Q_HEADS // N_KV_HEADS)
        qp = q_ref[:, pair * V7X_LANES:(pair + 1) * V7X_LANES]
        kk_g = kk[:, grp * V7X_LANES:(grp + 1) * V7X_LANES]
        vv_g = vv[:, grp * V7X_LANES:(grp + 1) * V7X_LANES]
        outs = []
        for sub in range(2):
            head = 2 * pair + sub
            keep = upper if sub else jnp.logical_not(upper)
            qm = jnp.where(keep, qp, jnp.zeros_like(qp))
            s = lax.dot_general(qm, kk_g, _NT, preferred_element_type=F32)
            slope = 2.0 ** (-8.0 * (head + 1) / N_Q_HEADS)
            s = jnp.where(valid, s * scale - slope * dist, NEG_INF)
            sk = sink_ref[head]
            m = jnp.maximum(jnp.max(s, axis=-1, keepdims=True), sk)
            e = jnp.exp(s - m)
            den = jnp.sum(e, axis=-1, keepdims=True) + jnp.exp(sk - m)
            p = (e / den).astype(BF16)
            outs.append(jnp.dot(p, vv_g, preferred_element_type=F32))
        o_pair = jnp.where(upper, outs[1], outs[0])
        o_ref[:, pair * V7X_LANES:(pair + 1) * V7X_LANES] = o_pair.astype(o_ref.dtype)


def _attention(sink, q3, k3, v3):
    b, s, _ = q3.shape
    nb = s // BLOCK
    cur = lambda bi, n: (bi, n, 0)
    prev = lambda bi, n: (bi, jnp.maximum(n - 1, 0), 0)
    return pl.pallas_call(
        _attn_kernel,
        grid=(b, nb),
        in_specs=[
            pl.BlockSpec(memory_space=pltpu.SMEM),
            pl.BlockSpec((None, BLOCK, Q_W), cur),
            pl.BlockSpec((None, BLOCK, 2 * KV_W), cur),
            pl.BlockSpec((None, BLOCK, 2 * KV_W), prev),
            pl.BlockSpec((None, BLOCK, 2 * KV_W), cur),
            pl.BlockSpec((None, BLOCK, 2 * KV_W), prev),
        ],
        out_specs=pl.BlockSpec((None, BLOCK, Q_W), cur),
        out_shape=jax.ShapeDtypeStruct((b, s, Q_W), BF16),
        compiler_params=_params("parallel", "parallel"),
        name="swa_attention",
    )(sink, q3, k3, k3, v3, v3)


def _lru_kernel(x_ref, gate_ref, cw_ref, cb_ref, wa_ref, ba_ref, wx_ref, bx_ref, lam_ref, o_ref):
    s_len = x_ref.shape[0]
    x = x_ref[...].astype(F32)
    row = lax.broadcasted_iota(I32, x.shape, 0)

    def shift(v, d, fill):
        return jnp.where(row >= d, pltpu.roll(v, d, axis=0), fill)

    cw = cw_ref[...]
    y = cb_ref[...] + shift(x, 3, 0.0) * cw[0:1]
    y = y + shift(x, 2, 0.0) * cw[1:2]
    y = y + shift(x, 1, 0.0) * cw[2:3]
    y = y + x * cw[3:4]
    yb = y.astype(BF16)
    r = _sigmoid(jnp.dot(yb, wa_ref[...], preferred_element_type=F32) + ba_ref[...])
    i = _sigmoid(jnp.dot(yb, wx_ref[...], preferred_element_type=F32) + bx_ref[...])
    z = -lam_ref[...]
    softplus = jnp.maximum(z, 0.0) + jnp.log1p(jnp.exp(-jnp.abs(z)))
    log_a = (-LRU_C) * r * softplus
    a = jnp.exp(log_a)
    z = 1.0 - a * a
    root = jnp.where(z > 0.0, z * lax.rsqrt(z), 0.0)
    b = root * (i * y)
    grp = V7X_SUBLANES
    a = a.reshape(s_len // grp, grp, -1)
    b = b.reshape(a.shape)
    sub = lax.broadcasted_iota(I32, a.shape, 1)
    d = 1
    while d < grp:
        inside = sub >= d
        b = a * jnp.where(inside, pltpu.roll(b, d, axis=1), 0.0) + b
        a = a * jnp.where(inside, pltpu.roll(a, d, axis=1), 1.0)
        d *= 2
    a = a.reshape(x.shape)
    b = b.reshape(x.shape)
    gate = _gelu(gate_ref[...].astype(F32))
    carry = jnp.zeros((1, x.shape[1]), F32)
    step = 2 * grp
    for t0 in range(0, s_len, step):
        hs = []
        for t in range(t0, t0 + step, grp):
            h = a[t:t + grp] * carry + b[t:t + grp]
            carry = h[grp - 1:grp]
            hs.append(h)
        o_ref[t0:t0 + step, :] = (jnp.concatenate(hs, axis=0) * gate[t0:t0 + step]).astype(o_ref.dtype)


def _rg_lru(lx3, lg3, conv_w, conv_b, wa, ba, wx, bx, lam):
    b, s, c = lx3.shape
    bd = LRU_BLOCK_DIM
    act = lambda bi, ci: (bi, 0, ci)
    vec = lambda bi, ci: (0, ci)
    mat = lambda bi, ci: (ci, 0, 0)
    return pl.pallas_call(
        _lru_kernel,
        grid=(b, c // bd),
        in_specs=[
            pl.BlockSpec((None, s, bd), act),
            pl.BlockSpec((None, s, bd), act),
            pl.BlockSpec((CONV_WIDTH, bd), vec),
            pl.BlockSpec((1, bd), vec),
            pl.BlockSpec((None, bd, bd), mat),
            pl.BlockSpec((1, bd), vec),
            pl.BlockSpec((None, bd, bd), mat),
            pl.BlockSpec((1, bd), vec),
            pl.BlockSpec((1, bd), vec),
        ],
        out_specs=pl.BlockSpec((None, s, bd), act),
        out_shape=jax.ShapeDtypeStruct((b, s, c), BF16),
        compiler_params=_params("parallel", "parallel"),
        name="rg_lru",
    )(lx3, lg3, conv_w, conv_b, wa, ba, wx, bx, lam)


def _merge_kernel(attn_ref, rec_ref, ga_ref, gl_ref, x_ref, wua_ref, wul_ref, wo_ref, g2_ref, wq_ref,
                  h_ref, xn_ref, qp_ref):
    up_a = jnp.dot(attn_ref[...], wua_ref[...], preferred_element_type=F32)
    up_l = jnp.dot(rec_ref[...], wul_ref[...], preferred_element_type=F32)
    merged = _sigmoid(ga_ref[...].astype(F32)) * up_a + _sigmoid(gl_ref[...].astype(F32)) * up_l
    h = x_ref[...] + jnp.dot(merged.astype(BF16), wo_ref[...], preferred_element_type=F32)
    h_ref[...] = h
    xn = _rms_norm(h, g2_ref[...]).astype(BF16)
    xn_ref[...] = xn
    qp_ref[...] = jnp.dot(xn, wq_ref[...], preferred_element_type=F32).astype(qp_ref.dtype)


def _merge(attn, rec, ga, gl, x2, wua, wul, wo, g2, wq, tm):
    t = x2.shape[0]
    nq = wq.shape[1]
    tok = lambda w: pl.BlockSpec((tm, w), lambda i: (i, 0))
    full = lambda a: pl.BlockSpec(a.shape, lambda i: (0, 0))
    return pl.pallas_call(
        _merge_kernel,
        grid=(t // tm,),
        in_specs=[tok(Q_W), tok(D_MODEL), tok(D_MODEL), tok(D_MODEL), tok(D_MODEL),
                  full(wua), full(wul), full(wo), full(g2), full(wq)],
        out_specs=[tok(D_MODEL), tok(D_MODEL), tok(nq)],
        out_shape=[jax.ShapeDtypeStruct((t, D_MODEL), F32),
                   jax.ShapeDtypeStruct((t, D_MODEL), BF16),
                   jax.ShapeDtypeStruct((t, nq), BF16)],
        compiler_params=_params("parallel"),
        name="merge_proj",
    )(attn, rec, ga, gl, x2, wua, wul, wo, g2, wq)


def _oddeven_merge(lo, hi, r):
    step = r * 2
    if step < hi - lo:
        yield from _oddeven_merge(lo, hi, step)
        yield from _oddeven_merge(lo + r, hi, step)
        yield from [(i, i + r) for i in range(lo + r, hi - r, step)]
    else:
        yield (lo, lo + r)


def _oddeven_merge_sort(lo, hi):
    if hi - lo >= 1:
        mid = lo + (hi - lo) // 2
        yield from _oddeven_merge_sort(lo, mid)
        yield from _oddeven_merge_sort(mid + 1, hi)
        yield from _oddeven_merge(lo, hi, 1)


_SORT16 = tuple(_oddeven_merge_sort(0, PEER_TOPK - 1))


def _precedes(a, b):
    va, pa = a
    vb, pb = b
    if isinstance(pa, int) and isinstance(pb, int):
        return (va >= vb) if pa < pb else (va > vb)
    return (va > vb) | ((va == vb) & (pa < pb))


def _first(a, b):
    if b is None:
        return a
    if a is None:
        return b
    return jnp.maximum(a[0], b[0]), jnp.where(_precedes(a, b), a[1], b[1])


def _exchange(a, b):
    if b is None:
        return a, None
    if a is None:
        return b, None
    c = _precedes(a, b)
    return ((jnp.maximum(a[0], b[0]), jnp.where(c, a[1], b[1])),
            (jnp.minimum(a[0], b[0]), jnp.where(c, b[1], a[1])))


def _sort16(items):
    items = list(items)
    for i, j in _SORT16:
        items[i], items[j] = _exchange(items[i], items[j])
    return items


def _merge16(xs, ys):
    n = PEER_TOPK
    xs = list(xs) + [None] * (n - len(xs))
    ys = list(ys) + [None] * (n - len(ys))
    h = [_first(xs[i], ys[n - 1 - i]) for i in range(n)]
    d = n // 2
    while d:
        for i in range(n):
            if not i & d:
                h[i], h[i + d] = _exchange(h[i], h[i + d])
        d //= 2
    return [x for x in h if x is not None]


def _top16_of_keys(s_ref):
    runs = []
    for g in range(PEER_KEYS // PEER_TOPK):
        keys = range(g * PEER_TOPK, (g + 1) * PEER_TOPK)
        runs.append(_sort16([(s_ref[pl.ds(k, PEER_HEADS, stride=PEER_KEYS), :], k) for k in keys]))
    while len(runs) > 1:
        runs = [_merge16(runs[i], runs[i + 1]) for i in range(0, len(runs), 2)]
    return runs[0]


_EXPERT_BITS = 14


def _topk_kernel(qp_ref, k1_ref, k2_ref, g_ref, e_ref, s1_scr, s2_scr):
    for h in range(PEER_HEADS):
        off = h * 2 * PEER_HALF
        rows = pl.ds(h * PEER_KEYS, PEER_KEYS)
        s1_scr[rows, :] = lax.dot_general(k1_ref[...], qp_ref[:, off:off + PEER_HALF], _NT,
                                          preferred_element_type=F32)
        s2_scr[rows, :] = lax.dot_general(k2_ref[...], qp_ref[:, off + PEER_HALF:off + 2 * PEER_HALF], _NT,
                                          preferred_element_type=F32)
    top1 = _top16_of_keys(s1_scr)
    top2 = _top16_of_keys(s2_scr)

    def pair(j, l):
        expert = top1[j][1] * PEER_KEYS + top2[l][1]
        return top1[j][0] + top2[l][0], expert + ((j * PEER_TOPK + l) << _EXPERT_BITS)

    half = PEER_TOPK // 2
    rows = [[pair(j, l) for l in range(PEER_TOPK // (j + 1))] for j in range(half)]
    col = [pair(j, 0) for j in range(half, PEER_TOPK)]
    small = _merge16(_merge16(rows[2], rows[3]),
                     _merge16(_merge16(rows[4], rows[5]), _merge16(rows[6], rows[7])))
    top = _merge16(rows[0], _merge16(_merge16(rows[1], col), small))
    ex = [jnp.exp(v - top[0][0]) for v, _ in top]
    den = ex[0]
    for x in ex[1:]:
        den = den + x
    gates = jnp.concatenate([x / den for x in ex], axis=0)
    experts = jnp.concatenate([p & ((1 << _EXPERT_BITS) - 1) for _, p in top], axis=0)
    g_ref[...] = gates.T
    e_ref[...] = experts.T


def _peer_topk(qp, k1, k2, tm):
    t = qp.shape[0]
    nsel = PEER_HEADS * PEER_TOPK
    tok = pl.BlockSpec((tm, nsel), lambda i: (i, 0))
    keys = pl.BlockSpec((PEER_KEYS, PEER_HALF), lambda i: (0, 0))
    return pl.pallas_call(
        _topk_kernel,
        grid=(t // tm,),
        in_specs=[pl.BlockSpec((tm, qp.shape[1]), lambda i: (i, 0)), keys, keys],
        out_specs=[tok, tok],
        out_shape=[jax.ShapeDtypeStruct((t, nsel), F32), jax.ShapeDtypeStruct((t, nsel), I32)],
        scratch_shapes=[pltpu.VMEM((PEER_HEADS * PEER_KEYS, tm), F32),
                        pltpu.VMEM((PEER_HEADS * PEER_KEYS, tm), F32)],
        compiler_params=_params("parallel"),
        name="peer_topk",
    )(qp, k1, k2)


def _wbuild_kernel(g_ref, e_ref, o_ref):
    tm = g_ref.shape[0]
    key = lax.broadcasted_iota(I32, (PEER_KEYS, g_ref.shape[1]), 0)

    def body(t, carry):
        g = g_ref[pl.ds(t, 1), :]
        e = e_ref[pl.ds(t, 1), :]
        a = e >> (_EXPERT_BITS // 2)
        b = e & (PEER_KEYS - 1)
        sel_a = jnp.where(key == a, g, 0.0).astype(BF16)
        sel_b = jnp.where(key == b, 1.0, 0.0).astype(BF16)
        w = lax.dot_general(sel_a, sel_b, _NT, preferred_element_type=F32)
        row0 = pl.multiple_of(t * V7X_SUBLANES, V7X_SUBLANES)
        o_ref[:, pl.ds(row0, V7X_SUBLANES), :] = w.reshape(
            PEER_KEYS // V7X_SUBLANES, V7X_SUBLANES, PEER_KEYS)
        return carry

    lax.fori_loop(0, tm, body, 0, unroll=32)


def _peer_wbuild(g, e, tm):
    t, nsel = g.shape
    tok = pl.BlockSpec((tm, nsel), lambda i: (i, 0))
    na = PEER_KEYS // V7X_SUBLANES
    return pl.pallas_call(
        _wbuild_kernel,
        grid=(t // tm,),
        in_specs=[tok, tok],
        out_specs=pl.BlockSpec((na, tm * V7X_SUBLANES, PEER_KEYS), lambda i: (0, i, 0)),
        out_shape=jax.ShapeDtypeStruct((na, t * V7X_SUBLANES, PEER_KEYS), F32),
        compiler_params=_params("parallel"),
        name="peer_wbuild",
    )(g, e)


def _ffn_kernel(x_ref, u_ref, v_ref, w_ref, y_ref):
    tm = x_ref.shape[0]

    @pl.when(pl.program_id(1) == 0)
    def _():
        y_ref[...] = jnp.zeros_like(y_ref)

    g = lax.dot_general(x_ref[...], u_ref[...], _NT, preferred_element_type=F32)
    hs = []
    for al in range(V7X_SUBLANES):
        wa = w_ref[pl.ds(al, tm, stride=V7X_SUBLANES), :]
        hs.append((wa * _gelu(g[:, al * PEER_KEYS:(al + 1) * PEER_KEYS])).astype(BF16))
    h = jnp.concatenate(hs, axis=1)
    y_ref[...] += jnp.dot(h, v_ref[...], preferred_element_type=F32)


def _peer_ffn(xn, u, v, w, tm):
    t = xn.shape[0]
    ne = V7X_SUBLANES * PEER_KEYS
    return pl.pallas_call(
        _ffn_kernel,
        grid=(t // tm, u.shape[0] // ne),
        in_specs=[
            pl.BlockSpec((tm, D_MODEL), lambda i, n: (i, 0)),
            pl.BlockSpec((ne, D_MODEL), lambda i, n: (n, 0)),
            pl.BlockSpec((ne, D_MODEL), lambda i, n: (n, 0)),
            pl.BlockSpec((None, tm * V7X_SUBLANES, PEER_KEYS), lambda i, n: (n, i, 0)),
        ],
        out_specs=pl.BlockSpec((tm, D_MODEL), lambda i, n: (i, 0)),
        out_shape=jax.ShapeDtypeStruct((t, D_MODEL), F32),
        compiler_params=_params("parallel", "arbitrary"),
        name="peer_ffn",
    )(xn, u, v, w)


def _final_kernel(h_ref, y_ref, p_ref, g3_ref, wg_ref, wp_ref, gf_ref, o_ref):
    h = h_ref[...] + y_ref[...]
    xn = _rms_norm(h, g3_ref[...]).astype(BF16)
    gate = _sigmoid(jnp.dot(xn, wg_ref[...], preferred_element_type=F32))
    proj = jnp.dot(p_ref[...].astype(BF16), wp_ref[...], preferred_element_type=F32)
    o_ref[...] = _rms_norm(h + gate * proj, gf_ref[...])


def _final(h1, y, p2, g3, wg, wp, gf, tm):
    t = h1.shape[0]
    tok = lambda w: pl.BlockSpec((tm, w), lambda i: (i, 0))
    full = lambda a: pl.BlockSpec(a.shape, lambda i: (0, 0))
    return pl.pallas_call(
        _final_kernel,
        grid=(t // tm,),
        in_specs=[tok(D_MODEL), tok(D_MODEL), tok(PLE_DIM), full(g3), full(wg), full(wp), full(gf)],
        out_specs=tok(D_MODEL),
        out_shape=jax.ShapeDtypeStruct((t, D_MODEL), F32),
        compiler_params=_params("parallel"),
        name="ple_final",
    )(h1, y, p2, g3, wg, wp, gf)


_TM_PROJ = 512
_TM_TOPK = 128
_TM_WBUILD = 128
_TM_FFN = 1024


def _double_heads(w):
    d = w.shape[0]
    w = w.reshape(d, N_KV_HEADS, 1, HEAD_DIM)
    return jnp.broadcast_to(w, (d, N_KV_HEADS, 2, HEAD_DIM)).reshape(d, 2 * KV_W)


def kernel(x, p, norm_mix_g, w_in, attn_sink, conv_w, conv_b, lru_wa, lru_ba, lru_wx, lru_bx, lru_lambda,
           w_up_attn, w_up_lru, w_o, norm_ffn_g, peer_wq, peer_k1, peer_k2, peer_u, peer_v, norm_ple_g,
           ple_w_gate, ple_w_proj, final_g):
    bsz, seq, d = x.shape
    t = bsz * seq
    assert p.shape[0] == 1, "single-layer block: the final norm follows layer 0"
    h = x.reshape(t, d)
    for l in range(1):
        row = lambda a: a.reshape(1, -1)
        wi = w_in[l]
        w_comb = jnp.concatenate(
            [wi[:, :Q_W], _double_heads(wi[:, Q_W:Q_W + KV_W]), _double_heads(wi[:, Q_W + KV_W:Q_W + 2 * KV_W]),
             wi[:, Q_W + 2 * KV_W:]], axis=1).astype(BF16)
        q, kd, vd, lx, lg, ga, gl = _in_proj(h, row(norm_mix_g[l]), w_comb, _TM_PROJ)
        attn = _attention(attn_sink[l], q.reshape(bsz, seq, -1), kd.reshape(bsz, seq, -1),
                          vd.reshape(bsz, seq, -1)).reshape(t, Q_W)
        rec = _rg_lru(lx.reshape(bsz, seq, d), lg.reshape(bsz, seq, d), conv_w[l], row(conv_b[l]),
                      lru_wa[l].astype(BF16), row(lru_ba[l]), lru_wx[l].astype(BF16), row(lru_bx[l]),
                      row(lru_lambda[l])).reshape(t, d)
        h1, xn2, qp = _merge(attn, rec, ga, gl, h, w_up_attn[l].astype(BF16), w_up_lru[l].astype(BF16),
                             w_o[l].astype(BF16), row(norm_ffn_g[l]), peer_wq[l].astype(BF16), _TM_PROJ)
        gates, experts = _peer_topk(qp, peer_k1[l].astype(BF16), peer_k2[l].astype(BF16), _TM_TOPK)
        w = _peer_wbuild(gates, experts, _TM_WBUILD)
        y = _peer_ffn(xn2, peer_u[l].astype(BF16), peer_v[l].astype(BF16), w, _TM_FFN)
        h = _final(h1, y, p[l].reshape(t, PLE_DIM), row(norm_ple_g[l]), ple_w_gate[l].astype(BF16),
                   ple_w_proj[l].astype(BF16), row(final_g), _TM_PROJ)
    return h.reshape(bsz, seq, d)
```

```python
import jax
import jax.numpy as jnp
from jax import lax
from jax.experimental import pallas as pl
from jax.experimental.pallas import tpu as pltpu

D_MODEL = 1024
PLE_DIM = 256
EPS = 1e-6
NEG_INF = -1e30
N_Q_HEADS = 8
N_KV_HEADS = 2
HEAD_DIM = 64
WINDOW = 128
BLOCK = 128
LRU_BLOCKS = 8
LRU_BLOCK_DIM = 128
CONV_WIDTH = 4
LRU_C = 8.0
PEER_HEADS = 8
PEER_KEYS = 128
PEER_HALF = 128
PEER_TOPK = 16
Q_W = N_Q_HEADS * HEAD_DIM
KV_W = N_KV_HEADS * HEAD_DIM

V7X_LANES = 128
V7X_SUBLANES = 8
V7X_VMEM_LIMIT_BYTES = 56 * 1024 * 1024

F32 = jnp.float32
BF16 = jnp.bfloat16
I32 = jnp.int32

_NT = (((1,), (1,)), ((), ()))


def _gelu(x):
    return 0.5 * x * (1.0 + jnp.tanh(0.7978845608028654 * (x + 0.044715 * (x * x * x))))


def _sigmoid(x):
    return 0.5 * (jnp.tanh(0.5 * x) + 1.0)


def _rms_norm(xf, g):
    ms = jnp.mean(xf * xf, axis=-1, keepdims=True)
    return xf * lax.rsqrt(ms + EPS) * g


def _params(*sem):
    return pltpu.CompilerParams(dimension_semantics=sem, vmem_limit_bytes=V7X_VMEM_LIMIT_BYTES)


_IN_COLS = (Q_W, 2 * KV_W, 2 * KV_W, D_MODEL, D_MODEL, D_MODEL, D_MODEL)


def _in_proj_kernel(x_ref, g_ref, w_ref, *out_refs):
    xn = _rms_norm(x_ref[...], g_ref[...]).astype(BF16)
    c0 = 0
    for ref, width in zip(out_refs, _IN_COLS):
        ref[...] = jnp.dot(xn, w_ref[:, c0:c0 + width], preferred_element_type=F32).astype(ref.dtype)
        c0 += width


def _in_proj(x2, g, w, tm):
    t = x2.shape[0]
    n_in = sum(_IN_COLS)
    return pl.pallas_call(
        _in_proj_kernel,
        grid=(t // tm,),
        in_specs=[
            pl.BlockSpec((tm, D_MODEL), lambda i: (i, 0)),
            pl.BlockSpec((1, D_MODEL), lambda i: (0, 0)),
            pl.BlockSpec((D_MODEL, n_in), lambda i: (0, 0)),
        ],
        out_specs=[pl.BlockSpec((tm, c), lambda i: (i, 0)) for c in _IN_COLS],
        out_shape=[jax.ShapeDtypeStruct((t, c), BF16) for c in _IN_COLS],
        compiler_params=_params("parallel"),
        name="in_proj",
    )(x2, g, w)


def _attn_kernel(sink_ref, q_ref, kc_ref, kp_ref, vc_ref, vp_ref, o_ref):
    n = pl.program_id(1)
    qi = lax.broadcasted_iota(I32, (BLOCK, 2 * BLOCK), 0)
    kj = lax.broadcasted_iota(I32, (BLOCK, 2 * BLOCK), 1)
    dist_i = qi + BLOCK - kj
    dist = dist_i.astype(F32)
    valid = (dist_i >= 0) & (dist_i < WINDOW) & ((n * BLOCK + kj - BLOCK) >= 0)
    kk = jnp.concatenate([kp_ref[...], kc_ref[...]], axis=0)
    vv = jnp.concatenate([vp_ref[...], vc_ref[...]], axis=0)
    lane = lax.broadcasted_iota(I32, (BLOCK, V7X_LANES), 1)
    upper = lane >= HEAD_DIM
    scale = HEAD_DIM ** -0.5
    for pair in range(N_Q_HEADS // 2):
        grp = (2 * pair) // (N_Q_HEADS // N_KV_HEADS)
        qp = q_ref[:, pair * V7X_LANES:(pair + 1) * V7X_LANES]
        kk_g = kk[:, grp * V7X_LANES:(grp + 1) * V7X_LANES]
        vv_g = vv[:, grp * V7X_LANES:(grp + 1) * V7X_LANES]
        outs = []
        for sub in range(2):
            head = 2 * pair + sub
            keep = upper if sub else jnp.logical_not(upper)
            qm = jnp.where(keep, qp, jnp.zeros_like(qp))
            s = lax.dot_general(qm, kk_g, _NT, preferred_element_type=F32)
            slope = 2.0 ** (-8.0 * (head + 1) / N_Q_HEADS)
            s = jnp.where(valid, s * scale - slope * dist, NEG_INF)
            sk = sink_ref[head]
            m = jnp.maximum(jnp.max(s, axis=-1, keepdims=True), sk)
            e = jnp.exp(s - m)
            den = jnp.sum(e, axis=-1, keepdims=True) + jnp.exp(sk - m)
            p = (e / den).astype(BF16)
            outs.append(jnp.dot(p, vv_g, preferred_element_type=F32))
        o_pair = jnp.where(upper, outs[1], outs[0])
        o_ref[:, pair * V7X_LANES:(pair + 1) * V7X_LANES] = o_pair.astype(o_ref.dtype)


def _attention(sink, q3, k3, v3):
    b, s, _ = q3.shape
    nb = s // BLOCK
    cur = lambda bi, n: (bi, n, 0)
    prev = lambda bi, n: (bi, jnp.maximum(n - 1, 0), 0)
    return pl.pallas_call(
        _attn_kernel,
        grid=(b, nb),
        in_specs=[
            pl.BlockSpec(memory_space=pltpu.SMEM),
            pl.BlockSpec((None, BLOCK, Q_W), cur),
            pl.BlockSpec((None, BLOCK, 2 * KV_W), cur),
            pl.BlockSpec((None, BLOCK, 2 * KV_W), prev),
            pl.BlockSpec((None, BLOCK, 2 * KV_W), cur),
            pl.BlockSpec((None, BLOCK, 2 * KV_W), prev),
        ],
        out_specs=pl.BlockSpec((None, BLOCK, Q_W), cur),
        out_shape=jax.ShapeDtypeStruct((b, s, Q_W), BF16),
        compiler_params=_params("parallel", "parallel"),
        name="swa_attention",
    )(sink, q3, k3, k3, v3, v3)


def _lru_kernel(x_ref, gate_ref, cw_ref, cb_ref, wa_ref, ba_ref, wx_ref, bx_ref, lam_ref, o_ref):
    s_len = x_ref.shape[0]
    x = x_ref[...].astype(F32)
    row = lax.broadcasted_iota(I32, x.shape, 0)

    def shift(v, d, fill):
        return jnp.where(row >= d, pltpu.roll(v, d, axis=0), fill)

    cw = cw_ref[...]
    y = cb_ref[...] + shift(x, 3, 0.0) * cw[0:1]
    y = y + shift(x, 2, 0.0) * cw[1:2]
    y = y + shift(x, 1, 0.0) * cw[2:3]
    y = y + x * cw[3:4]
    yb = y.astype(BF16)
    r = _sigmoid(jnp.dot(yb, wa_ref[...], preferred_element_type=F32) + ba_ref[...])
    i = _sigmoid(jnp.dot(yb, wx_ref[...], preferred_element_type=F32) + bx_ref[...])
    z = -lam_ref[...]
    softplus = jnp.maximum(z, 0.0) + jnp.log1p(jnp.exp(-jnp.abs(z)))
    log_a = (-LRU_C) * r * softplus
    a = jnp.exp(log_a)
    z = 1.0 - a * a
    root = jnp.where(z > 0.0, z * lax.rsqrt(z), 0.0)
    b = root * (i * y)
    grp = V7X_SUBLANES
    a = a.reshape(s_len // grp, grp, -1)
    b = b.reshape(a.shape)
    sub = lax.broadcasted_iota(I32, a.shape, 1)
    d = 1
    while d < grp:
        inside = sub >= d
        b = a * jnp.where(inside, pltpu.roll(b, d, axis=1), 0.0) + b
        a = a * jnp.where(inside, pltpu.roll(a, d, axis=1), 1.0)
        d *= 2
    a = a.reshape(x.shape)
    b = b.reshape(x.shape)
    gate = _gelu(gate_ref[...].astype(F32))
    carry = jnp.zeros((1, x.shape[1]), F32)
    step = 2 * grp
    for t0 in range(0, s_len, step):
        hs = []
        for t in range(t0, t0 + step, grp):
            h = a[t:t + grp] * carry + b[t:t + grp]
            carry = h[grp - 1:grp]
            hs.append(h)
        o_ref[t0:t0 + step, :] = (jnp.concatenate(hs, axis=0) * gate[t0:t0 + step]).astype(o_ref.dtype)


def _rg_lru(lx3, lg3, conv_w, conv_b, wa, ba, wx, bx, lam):
    b, s, c = lx3.shape
    bd = LRU_BLOCK_DIM
    act = lambda bi, ci: (bi, 0, ci)
    vec = lambda bi, ci: (0, ci)
    mat = lambda bi, ci: (ci, 0, 0)
    return pl.pallas_call(
        _lru_kernel,
        grid=(b, c // bd),
        in_specs=[
            pl.BlockSpec((None, s, bd), act),
            pl.BlockSpec((None, s, bd), act),
            pl.BlockSpec((CONV_WIDTH, bd), vec),
            pl.BlockSpec((1, bd), vec),
            pl.BlockSpec((None, bd, bd), mat),
            pl.BlockSpec((1, bd), vec),
            pl.BlockSpec((None, bd, bd), mat),
            pl.BlockSpec((1, bd), vec),
            pl.BlockSpec((1, bd), vec),
        ],
        out_specs=pl.BlockSpec((None, s, bd), act),
        out_shape=jax.ShapeDtypeStruct((b, s, c), BF16),
        compiler_params=_params("parallel", "parallel"),
        name="rg_lru",
    )(lx3, lg3, conv_w, conv_b, wa, ba, wx, bx, lam)


def _merge_kernel(attn_ref, rec_ref, ga_ref, gl_ref, x_ref, wua_ref, wul_ref, wo_ref, g2_ref, wq_ref,
                  h_ref, xn_ref, qp_ref):
    up_a = jnp.dot(attn_ref[...], wua_ref[...], preferred_element_type=F32)
    up_l = jnp.dot(rec_ref[...], wul_ref[...], preferred_element_type=F32)
    merged = _sigmoid(ga_ref[...].astype(F32)) * up_a + _sigmoid(gl_ref[...].astype(F32)) * up_l
    h = x_ref[...] + jnp.dot(merged.astype(BF16), wo_ref[...], preferred_element_type=F32)
    h_ref[...] = h
    xn = _rms_norm(h, g2_ref[...]).astype(BF16)
    xn_ref[...] = xn
    qp_ref[...] = jnp.dot(xn, wq_ref[...], preferred_element_type=F32).astype(qp_ref.dtype)


def _merge(attn, rec, ga, gl, x2, wua, wul, wo, g2, wq, tm):
    t = x2.shape[0]
    nq = wq.shape[1]
    tok = lambda w: pl.BlockSpec((tm, w), lambda i: (i, 0))
    full = lambda a: pl.BlockSpec(a.shape, lambda i: (0, 0))
    return pl.pallas_call(
        _merge_kernel,
        grid=(t // tm,),
        in_specs=[tok(Q_W), tok(D_MODEL), tok(D_MODEL), tok(D_MODEL), tok(D_MODEL),
                  full(wua), full(wul), full(wo), full(g2), full(wq)],
        out_specs=[tok(D_MODEL), tok(D_MODEL), tok(nq)],
        out_shape=[jax.ShapeDtypeStruct((t, D_MODEL), F32),
                   jax.ShapeDtypeStruct((t, D_MODEL), BF16),
                   jax.ShapeDtypeStruct((t, nq), BF16)],
        compiler_params=_params("parallel"),
        name="merge_proj",
    )(attn, rec, ga, gl, x2, wua, wul, wo, g2, wq)


def _oddeven_merge(lo, hi, r):
    step = r * 2
    if step < hi - lo:
        yield from _oddeven_merge(lo, hi, step)
        yield from _oddeven_merge(lo + r, hi, step)
        yield from [(i, i + r) for i in range(lo + r, hi - r, step)]
    else:
        yield (lo, lo + r)


def _oddeven_merge_sort(lo, hi):
    if hi - lo >= 1:
        mid = lo + (hi - lo) // 2
        yield from _oddeven_merge_sort(lo, mid)
        yield from _oddeven_merge_sort(mid + 1, hi)
        yield from _oddeven_merge(lo, hi, 1)


_SORT16 = tuple(_oddeven_merge_sort(0, PEER_TOPK - 1))


def _precedes(a, b):
    va, pa = a
    vb, pb = b
    if isinstance(pa, int) and isinstance(pb, int):
        return (va >= vb) if pa < pb else (va > vb)
    return (va > vb) | ((va == vb) & (pa < pb))


def _first(a, b):
    if b is None:
        return a
    if a is None:
        return b
    return jnp.maximum(a[0], b[0]), jnp.where(_precedes(a, b), a[1], b[1])


def _no_tick():
    return None


def _exchange(a, b, tick=_no_tick):
    if b is None:
        return a, None
    if a is None:
        return b, None
    c = _precedes(a, b)
    first_payload = jnp.where(c, a[1], b[1])
    zero = tick()
    if zero is not None:
        first_payload = first_payload + zero
    return ((jnp.maximum(a[0], b[0]), first_payload),
            (jnp.minimum(a[0], b[0]), jnp.where(c, b[1], a[1])))


def _sort16(items, tick=_no_tick):
    items = list(items)
    for i, j in _SORT16:
        items[i], items[j] = _exchange(items[i], items[j], tick)
    return items


def _merge16(xs, ys, tick=_no_tick):
    n = PEER_TOPK
    xs = list(xs) + [None] * (n - len(xs))
    ys = list(ys) + [None] * (n - len(ys))
    h = [_first(xs[i], ys[n - 1 - i]) for i in range(n)]
    d = n // 2
    while d:
        for i in range(n):
            if not i & d:
                h[i], h[i + d] = _exchange(h[i], h[i + d], tick)
        d //= 2
    return [x for x in h if x is not None]


def _top16_of_keys(s_ref, tick=_no_tick):
    runs = []
    for g in range(PEER_KEYS // PEER_TOPK):
        keys = range(g * PEER_TOPK, (g + 1) * PEER_TOPK)
        runs.append(_sort16([(s_ref[pl.ds(k, PEER_HEADS, stride=PEER_KEYS), :], k) for k in keys], tick))
    while len(runs) > 1:
        runs = [_merge16(runs[i], runs[i + 1], tick) for i in range(0, len(runs), 2)]
    return runs[0]


_EXPERT_BITS = 14
_EXCHANGES_PER_TOKEN = 12
_TOKENS_IN_FLIGHT = 12


def _retrieve(qp_ref, k1_ref, k2_ref, s1_scr, s2_scr, tick=_no_tick):
    for h in range(PEER_HEADS):
        off = h * 2 * PEER_HALF
        rows = pl.ds(h * PEER_KEYS, PEER_KEYS)
        s1_scr[rows, :] = lax.dot_general(k1_ref[...], qp_ref[:, off:off + PEER_HALF], _NT,
                                          preferred_element_type=F32)
        s2_scr[rows, :] = lax.dot_general(k2_ref[...], qp_ref[:, off + PEER_HALF:off + 2 * PEER_HALF], _NT,
                                          preferred_element_type=F32)
    top1 = _top16_of_keys(s1_scr, tick)
    top2 = _top16_of_keys(s2_scr, tick)

    def pair(j, l):
        expert = top1[j][1] * PEER_KEYS + top2[l][1]
        return top1[j][0] + top2[l][0], expert + ((j * PEER_TOPK + l) << _EXPERT_BITS)

    half = PEER_TOPK // 2
    rows = [[pair(j, l) for l in range(PEER_TOPK // (j + 1))] for j in range(half)]
    col = [pair(j, 0) for j in range(half, PEER_TOPK)]
    merge = lambda xs, ys: _merge16(xs, ys, tick)
    small = merge(merge(rows[2], rows[3]), merge(merge(rows[4], rows[5]), merge(rows[6], rows[7])))
    top = merge(rows[0], merge(merge(rows[1], col), small))
    ex = [jnp.exp(v - top[0][0]) for v, _ in top]
    den = ex[0]
    for x in ex[1:]:
        den = den + x
    gates = jnp.concatenate([x / den for x in ex], axis=0)
    experts = jnp.concatenate([p & ((1 << _EXPERT_BITS) - 1) for _, p in top], axis=0)
    return gates.T, experts.T


def _gate_matrix(g, e, key):
    a = e >> (_EXPERT_BITS // 2)
    b = e & (PEER_KEYS - 1)
    sel_a = jnp.where(key == a, g, 0.0).astype(BF16)
    sel_b = jnp.where(key == b, 1.0, 0.0).astype(BF16)
    return lax.dot_general(sel_a, sel_b, _NT, preferred_element_type=F32)


def _select_kernel(qp_ref, k1_ref, k2_ref, w_ref, s1_scr, s2_scr, g_scr, e_scr):
    tm = qp_ref.shape[0]

    @pl.when(pl.program_id(0) == 0)
    def _():
        g_scr[...] = jnp.zeros_like(g_scr)
        e_scr[...] = jnp.zeros_like(e_scr)

    key = lax.broadcasted_iota(I32, (PEER_KEYS, g_scr.shape[1]), 0)
    done = []
    ticks = [0]

    def gate_matrix_of_next_token():
        t = len(done)
        w = _gate_matrix(g_scr[t:t + 1, :], e_scr[t:t + 1, :], key)
        w_ref[:, t * V7X_SUBLANES:(t + 1) * V7X_SUBLANES, :] = w.reshape(
            PEER_KEYS // V7X_SUBLANES, V7X_SUBLANES, PEER_KEYS)
        done.append(w)

    def tick():
        ticks[0] += 1
        if ticks[0] % _EXCHANGES_PER_TOKEN:
            return None
        if len(done) < tm:
            gate_matrix_of_next_token()
        waited = ticks[0] // _EXCHANGES_PER_TOKEN - 1 - _TOKENS_IN_FLIGHT
        if not 0 <= waited < len(done):
            return None
        bits = lax.bitcast_convert_type(done[waited][:V7X_SUBLANES], jnp.uint32)
        return lax.bitcast_convert_type((bits >> 16) >> 16, I32)

    gates, experts = _retrieve(qp_ref, k1_ref, k2_ref, s1_scr, s2_scr, tick)
    while len(done) < tm:
        gate_matrix_of_next_token()
    g_scr[...] = gates
    e_scr[...] = experts


def _peer_select(qp, k1, k2, tm):
    t = qp.shape[0]
    nt = t // tm
    nsel = PEER_HEADS * PEER_TOPK
    na = PEER_KEYS // V7X_SUBLANES
    keys = pl.BlockSpec((PEER_KEYS, PEER_HALF), lambda i: (0, 0))
    return pl.pallas_call(
        _select_kernel,
        grid=(nt + 1,),
        in_specs=[pl.BlockSpec((tm, qp.shape[1]), lambda i: (jnp.minimum(i, nt - 1), 0)), keys, keys],
        out_specs=pl.BlockSpec((na, tm * V7X_SUBLANES, PEER_KEYS), lambda i: (0, jnp.maximum(i - 1, 0), 0)),
        out_shape=jax.ShapeDtypeStruct((na, t * V7X_SUBLANES, PEER_KEYS), F32),
        scratch_shapes=[pltpu.VMEM((PEER_HEADS * PEER_KEYS, tm), F32),
                        pltpu.VMEM((PEER_HEADS * PEER_KEYS, tm), F32),
                        pltpu.VMEM((tm, nsel), F32), pltpu.VMEM((tm, nsel), I32)],
        compiler_params=_params("arbitrary"),
        name="peer_select",
    )(qp, k1, k2)


def _ffn_kernel(x_ref, u_ref, v_ref, w_ref, y_ref):
    tm = x_ref.shape[0]

    @pl.when(pl.program_id(1) == 0)
    def _():
        y_ref[...] = jnp.zeros_like(y_ref)

    g = lax.dot_general(x_ref[...], u_ref[...], _NT, preferred_element_type=F32)
    hs = []
    for al in range(V7X_SUBLANES):
        wa = w_ref[pl.ds(al, tm, stride=V7X_SUBLANES), :]
        hs.append((wa * _gelu(g[:, al * PEER_KEYS:(al + 1) * PEER_KEYS])).astype(BF16))
    h = jnp.concatenate(hs, axis=1)
    y_ref[...] += jnp.dot(h, v_ref[...], preferred_element_type=F32)


def _peer_ffn(xn, u, v, w, tm):
    t = xn.shape[0]
    ne = V7X_SUBLANES * PEER_KEYS
    return pl.pallas_call(
        _ffn_kernel,
        grid=(t // tm, u.shape[0] // ne),
        in_specs=[
            pl.BlockSpec((tm, D_MODEL), lambda i, n: (i, 0)),
            pl.BlockSpec((ne, D_MODEL), lambda i, n: (n, 0)),
            pl.BlockSpec((ne, D_MODEL), lambda i, n: (n, 0)),
            pl.BlockSpec((None, tm * V7X_SUBLANES, PEER_KEYS), lambda i, n: (n, i, 0)),
        ],
        out_specs=pl.BlockSpec((tm, D_MODEL), lambda i, n: (i, 0)),
        out_shape=jax.ShapeDtypeStruct((t, D_MODEL), F32),
        compiler_params=_params("parallel", "arbitrary"),
        name="peer_ffn",
    )(xn, u, v, w)


def _final_kernel(h_ref, y_ref, p_ref, g3_ref, wg_ref, wp_ref, gf_ref, o_ref):
    h = h_ref[...] + y_ref[...]
    xn = _rms_norm(h, g3_ref[...]).astype(BF16)
    gate = _sigmoid(jnp.dot(xn, wg_ref[...], preferred_element_type=F32))
    proj = jnp.dot(p_ref[...].astype(BF16), wp_ref[...], preferred_element_type=F32)
    o_ref[...] = _rms_norm(h + gate * proj, gf_ref[...])


def _final(h1, y, p2, g3, wg, wp, gf, tm):
    t = h1.shape[0]
    tok = lambda w: pl.BlockSpec((tm, w), lambda i: (i, 0))
    full = lambda a: pl.BlockSpec(a.shape, lambda i: (0, 0))
    return pl.pallas_call(
        _final_kernel,
        grid=(t // tm,),
        in_specs=[tok(D_MODEL), tok(D_MODEL), tok(PLE_DIM), full(g3), full(wg), full(wp), full(gf)],
        out_specs=tok(D_MODEL),
        out_shape=jax.ShapeDtypeStruct((t, D_MODEL), F32),
        compiler_params=_params("parallel"),
        name="ple_final",
    )(h1, y, p2, g3, wg, wp, gf)


_TM_PROJ = 512
_TM_SELECT = 128
_TM_FFN = 1024


def _double_heads(w):
    d = w.shape[0]
    w = w.reshape(d, N_KV_HEADS, 1, HEAD_DIM)
    return jnp.broadcast_to(w, (d, N_KV_HEADS, 2, HEAD_DIM)).reshape(d, 2 * KV_W)


def kernel(x, p, norm_mix_g, w_in, attn_sink, conv_w, conv_b, lru_wa, lru_ba, lru_wx, lru_bx, lru_lambda,
           w_up_attn, w_up_lru, w_o, norm_ffn_g, peer_wq, peer_k1, peer_k2, peer_u, peer_v, norm_ple_g,
           ple_w_gate, ple_w_proj, final_g):
    bsz, seq, d = x.shape
    t = bsz * seq
    assert p.shape[0] == 1, "single-layer block: the final norm follows layer 0"
    h = x.reshape(t, d)
    for l in range(1):
        row = lambda a: a.reshape(1, -1)
        wi = w_in[l]
        w_comb = jnp.concatenate(
            [wi[:, :Q_W], _double_heads(wi[:, Q_W:Q_W + KV_W]), _double_heads(wi[:, Q_W + KV_W:Q_W + 2 * KV_W]),
             wi[:, Q_W + 2 * KV_W:]], axis=1).astype(BF16)
        q, kd, vd, lx, lg, ga, gl = _in_proj(h, row(norm_mix_g[l]), w_comb, _TM_PROJ)
        attn = _attention(attn_sink[l], q.reshape(bsz, seq, -1), kd.reshape(bsz, seq, -1),
                          vd.reshape(bsz, seq, -1)).reshape(t, Q_W)
        rec = _rg_lru(lx.reshape(bsz, seq, d), lg.reshape(bsz, seq, d), conv_w[l], row(conv_b[l]),
                      lru_wa[l].astype(BF16), row(lru_ba[l]), lru_wx[l].astype(BF16), row(lru_bx[l]),
                      row(lru_lambda[l])).reshape(t, d)
        h1, xn2, qp = _merge(attn, rec, ga, gl, h, w_up_attn[l].astype(BF16), w_up_lru[l].astype(BF16),
                             w_o[l].astype(BF16), row(norm_ffn_g[l]), peer_wq[l].astype(BF16), _TM_PROJ)
        w = _peer_select(qp, peer_k1[l].astype(BF16), peer_k2[l].astype(BF16), _TM_SELECT)
        y = _peer_ffn(xn2, peer_u[l].astype(BF16), peer_v[l].astype(BF16), w, _TM_FFN)
        h = _final(h1, y, p[l].reshape(t, PLE_DIM), row(norm_ple_g[l]), ple_w_gate[l].astype(BF16),
                   ple_w_proj[l].astype(BF16), row(final_g), _TM_PROJ)
    return h.reshape(bsz, seq, d)
```

```python
import jax
import jax.numpy as jnp
from jax import lax
from jax.experimental import pallas as pl
from jax.experimental.pallas import tpu as pltpu

D_MODEL = 1024
PLE_DIM = 256
EPS = 1e-6
NEG_INF = -1e30
N_Q_HEADS = 8
N_KV_HEADS = 2
HEAD_DIM = 64
WINDOW = 128
BLOCK = 128
LRU_BLOCKS = 8
LRU_BLOCK_DIM = 128
CONV_WIDTH = 4
LRU_C = 8.0
PEER_HEADS = 8
PEER_KEYS = 128
PEER_HALF = 128
PEER_TOPK = 16
Q_W = N_Q_HEADS * HEAD_DIM
KV_W = N_KV_HEADS * HEAD_DIM

V7X_LANES = 128
V7X_SUBLANES = 8
V7X_VMEM_LIMIT_BYTES = 56 * 1024 * 1024

F32 = jnp.float32
BF16 = jnp.bfloat16
I32 = jnp.int32

_NT = (((1,), (1,)), ((), ()))


def _gelu(x):
    return 0.5 * x * (1.0 + jnp.tanh(0.7978845608028654 * (x + 0.044715 * (x * x * x))))


def _sigmoid(x):
    return 0.5 * (jnp.tanh(0.5 * x) + 1.0)


def _rms_norm(xf, g):
    ms = jnp.mean(xf * xf, axis=-1, keepdims=True)
    return xf * lax.rsqrt(ms + EPS) * g


def _params(*sem):
    return pltpu.CompilerParams(dimension_semantics=sem, vmem_limit_bytes=V7X_VMEM_LIMIT_BYTES)


_IN_COLS = (Q_W, 2 * KV_W, 2 * KV_W, D_MODEL, D_MODEL, D_MODEL, D_MODEL)


def _in_proj_kernel(x_ref, g_ref, w_ref, table_ref, *out_refs):
    *proj_refs, table_bf16_ref = out_refs
    xn = _rms_norm(x_ref[...], g_ref[...]).astype(BF16)
    c0 = 0
    for ref, width in zip(proj_refs, _IN_COLS):
        ref[...] = jnp.dot(xn, w_ref[:, c0:c0 + width], preferred_element_type=F32).astype(ref.dtype)
        c0 += width
    table_bf16_ref[...] = table_ref[...].astype(BF16)


def _table_specs(table, steps):
    rows = table.shape[0] // steps
    spec = pl.BlockSpec((rows, table.shape[1]), lambda i: (i, 0))
    return spec, jax.ShapeDtypeStruct(table.shape, BF16)


def _in_proj(x2, g, w, table, tm):
    t = x2.shape[0]
    n_in = sum(_IN_COLS)
    table_spec, table_shape = _table_specs(table, t // tm)
    return pl.pallas_call(
        _in_proj_kernel,
        grid=(t // tm,),
        in_specs=[
            pl.BlockSpec((tm, D_MODEL), lambda i: (i, 0)),
            pl.BlockSpec((1, D_MODEL), lambda i: (0, 0)),
            pl.BlockSpec((D_MODEL, n_in), lambda i: (0, 0)),
            table_spec,
        ],
        out_specs=[pl.BlockSpec((tm, c), lambda i: (i, 0)) for c in _IN_COLS] + [table_spec],
        out_shape=[jax.ShapeDtypeStruct((t, c), BF16) for c in _IN_COLS] + [table_shape],
        compiler_params=_params("parallel"),
        name="in_proj",
    )(x2, g, w, table)


def _attn_kernel(sink_ref, q_ref, kc_ref, kp_ref, vc_ref, vp_ref, o_ref):
    n = pl.program_id(1)
    qi = lax.broadcasted_iota(I32, (BLOCK, 2 * BLOCK), 0)
    kj = lax.broadcasted_iota(I32, (BLOCK, 2 * BLOCK), 1)
    dist_i = qi + BLOCK - kj
    dist = dist_i.astype(F32)
    valid = (dist_i >= 0) & (dist_i < WINDOW) & ((n * BLOCK + kj - BLOCK) >= 0)
    kk = jnp.concatenate([kp_ref[...], kc_ref[...]], axis=0)
    vv = jnp.concatenate([vp_ref[...], vc_ref[...]], axis=0)
    lane = lax.broadcasted_iota(I32, (BLOCK, V7X_LANES), 1)
    upper = lane >= HEAD_DIM
    scale = HEAD_DIM ** -0.5
    for pair in range(N_Q_HEADS // 2):
        grp = (2 * pair) // (N_Q_HEADS // N_KV_HEADS)
        qp = q_ref[:, pair * V7X_LANES:(pair + 1) * V7X_LANES]
        kk_g = kk[:, grp * V7X_LANES:(grp + 1) * V7X_LANES]
        vv_g = vv[:, grp * V7X_LANES:(grp + 1) * V7X_LANES]
        outs = []
        for sub in range(2):
            head = 2 * pair + sub
            keep = upper if sub else jnp.logical_not(upper)
            qm = jnp.where(keep, qp, jnp.zeros_like(qp))
            s = lax.dot_general(qm, kk_g, _NT, preferred_element_type=F32)
            slope = 2.0 ** (-8.0 * (head + 1) / N_Q_HEADS)
            s = jnp.where(valid, s * scale - slope * dist, NEG_INF)
            sk = sink_ref[head]
            m = jnp.maximum(jnp.max(s, axis=-1, keepdims=True), sk)
            e = jnp.exp(s - m)
            den = jnp.sum(e, axis=-1, keepdims=True) + jnp.exp(sk - m)
            p = (e * (1.0 / den)).astype(BF16)
            outs.append(jnp.dot(p, vv_g, preferred_element_type=F32))
        o_pair = jnp.where(upper, outs[1], outs[0])
        o_ref[:, pair * V7X_LANES:(pair + 1) * V7X_LANES] = o_pair.astype(o_ref.dtype)


def _attention(sink, q3, k3, v3):
    b, s, _ = q3.shape
    nb = s // BLOCK
    cur = lambda bi, n: (bi, n, 0)
    prev = lambda bi, n: (bi, jnp.maximum(n - 1, 0), 0)
    return pl.pallas_call(
        _attn_kernel,
        grid=(b, nb),
        in_specs=[
            pl.BlockSpec(memory_space=pltpu.SMEM),
            pl.BlockSpec((None, BLOCK, Q_W), cur),
            pl.BlockSpec((None, BLOCK, 2 * KV_W), cur),
            pl.BlockSpec((None, BLOCK, 2 * KV_W), prev),
            pl.BlockSpec((None, BLOCK, 2 * KV_W), cur),
            pl.BlockSpec((None, BLOCK, 2 * KV_W), prev),
        ],
        out_specs=pl.BlockSpec((None, BLOCK, Q_W), cur),
        out_shape=jax.ShapeDtypeStruct((b, s, Q_W), BF16),
        compiler_params=_params("parallel", "parallel"),
        name="swa_attention",
    )(sink, q3, k3, k3, v3, v3)


def _lru_kernel(x_ref, gate_ref, cw_ref, cb_ref, wa_ref, ba_ref, wx_ref, bx_ref, lam_ref, o_ref):
    s_len = x_ref.shape[0]
    x = x_ref[...].astype(F32)
    row = lax.broadcasted_iota(I32, x.shape, 0)

    def shift(v, d, fill):
        return jnp.where(row >= d, pltpu.roll(v, d, axis=0), fill)

    cw = cw_ref[...]
    y = cb_ref[...] + shift(x, 3, 0.0) * cw[0:1]
    y = y + shift(x, 2, 0.0) * cw[1:2]
    y = y + shift(x, 1, 0.0) * cw[2:3]
    y = y + x * cw[3:4]
    yb = y.astype(BF16)
    r = _sigmoid(jnp.dot(yb, wa_ref[...], preferred_element_type=F32) + ba_ref[...])
    i = _sigmoid(jnp.dot(yb, wx_ref[...], preferred_element_type=F32) + bx_ref[...])
    z = -lam_ref[...]
    softplus = jnp.maximum(z, 0.0) + jnp.log1p(jnp.exp(-jnp.abs(z)))
    log_a = (-LRU_C) * r * softplus
    a = jnp.exp(log_a)
    z = 1.0 - a * a
    root = jnp.where(z > 0.0, z * lax.rsqrt(z), 0.0)
    b = root * (i * y)
    grp = V7X_SUBLANES
    a = a.reshape(s_len // grp, grp, -1)
    b = b.reshape(a.shape)
    sub = lax.broadcasted_iota(I32, a.shape, 1)
    d = 1
    while d < grp:
        inside = sub >= d
        b = a * jnp.where(inside, pltpu.roll(b, d, axis=1), 0.0) + b
        a = a * jnp.where(inside, pltpu.roll(a, d, axis=1), 1.0)
        d *= 2
    a = a.reshape(x.shape)
    b = b.reshape(x.shape)
    gate = _gelu(gate_ref[...].astype(F32))
    carry = jnp.zeros((1, x.shape[1]), F32)
    step = 2 * grp
    for t0 in range(0, s_len, step):
        hs = []
        for t in range(t0, t0 + step, grp):
            h = a[t:t + grp] * carry + b[t:t + grp]
            carry = h[grp - 1:grp]
            hs.append(h)
        o_ref[t0:t0 + step, :] = (jnp.concatenate(hs, axis=0) * gate[t0:t0 + step]).astype(o_ref.dtype)


def _rg_lru(lx3, lg3, conv_w, conv_b, wa, ba, wx, bx, lam):
    b, s, c = lx3.shape
    bd = LRU_BLOCK_DIM
    act = lambda bi, ci: (bi, 0, ci)
    vec = lambda bi, ci: (0, ci)
    mat = lambda bi, ci: (ci, 0, 0)
    return pl.pallas_call(
        _lru_kernel,
        grid=(b, c // bd),
        in_specs=[
            pl.BlockSpec((None, s, bd), act),
            pl.BlockSpec((None, s, bd), act),
            pl.BlockSpec((CONV_WIDTH, bd), vec),
            pl.BlockSpec((1, bd), vec),
            pl.BlockSpec((None, bd, bd), mat),
            pl.BlockSpec((1, bd), vec),
            pl.BlockSpec((None, bd, bd), mat),
            pl.BlockSpec((1, bd), vec),
            pl.BlockSpec((1, bd), vec),
        ],
        out_specs=pl.BlockSpec((None, s, bd), act),
        out_shape=jax.ShapeDtypeStruct((b, s, c), BF16),
        compiler_params=_params("parallel", "parallel"),
        name="rg_lru",
    )(lx3, lg3, conv_w, conv_b, wa, ba, wx, bx, lam)


def _merge_kernel(attn_ref, rec_ref, ga_ref, gl_ref, x_ref, wua_ref, wul_ref, wo_ref, g2_ref, wq_ref,
                  table_ref, h_ref, xn_ref, qp_ref, table_bf16_ref):
    table_bf16_ref[...] = table_ref[...].astype(BF16)
    up_a = jnp.dot(attn_ref[...], wua_ref[...], preferred_element_type=F32)
    up_l = jnp.dot(rec_ref[...], wul_ref[...], preferred_element_type=F32)
    merged = _sigmoid(ga_ref[...].astype(F32)) * up_a + _sigmoid(gl_ref[...].astype(F32)) * up_l
    h = x_ref[...] + jnp.dot(merged.astype(BF16), wo_ref[...], preferred_element_type=F32)
    h_ref[...] = h
    xn = _rms_norm(h, g2_ref[...]).astype(BF16)
    xn_ref[...] = xn
    qp_ref[...] = jnp.dot(xn, wq_ref[...], preferred_element_type=F32).astype(qp_ref.dtype)


def _merge(attn, rec, ga, gl, x2, wua, wul, wo, g2, wq, table, tm):
    t = x2.shape[0]
    nq = wq.shape[1]
    tok = lambda w: pl.BlockSpec((tm, w), lambda i: (i, 0))
    full = lambda a: pl.BlockSpec(a.shape, lambda i: (0, 0))
    table_spec, table_shape = _table_specs(table, t // tm)
    return pl.pallas_call(
        _merge_kernel,
        grid=(t // tm,),
        in_specs=[tok(Q_W), tok(D_MODEL), tok(D_MODEL), tok(D_MODEL), tok(D_MODEL),
                  full(wua), full(wul), full(wo), full(g2), full(wq), table_spec],
        out_specs=[tok(D_MODEL), tok(D_MODEL), tok(nq), table_spec],
        out_shape=[jax.ShapeDtypeStruct((t, D_MODEL), F32),
                   jax.ShapeDtypeStruct((t, D_MODEL), BF16),
                   jax.ShapeDtypeStruct((t, nq), BF16),
                   table_shape],
        compiler_params=_params("parallel"),
        name="merge_proj",
    )(attn, rec, ga, gl, x2, wua, wul, wo, g2, wq, table)


def _oddeven_merge(lo, hi, r):
    step = r * 2
    if step < hi - lo:
        yield from _oddeven_merge(lo, hi, step)
        yield from _oddeven_merge(lo + r, hi, step)
        yield from [(i, i + r) for i in range(lo + r, hi - r, step)]
    else:
        yield (lo, lo + r)


def _oddeven_merge_sort(lo, hi):
    if hi - lo >= 1:
        mid = lo + (hi - lo) // 2
        yield from _oddeven_merge_sort(lo, mid)
        yield from _oddeven_merge_sort(mid + 1, hi)
        yield from _oddeven_merge(lo, hi, 1)


_SORT16 = tuple(_oddeven_merge_sort(0, PEER_TOPK - 1))


def _no_tick():
    return None


class _Network:
    def __init__(self, tick=_no_tick):
        self.tick = tick

    def precedes(self, a, b):
        va, pa = a
        vb, pb = b
        if isinstance(pa, int) and isinstance(pb, int):
            return (va >= vb) if pa < pb else (va > vb)
        return (va > vb) | ((va == vb) & (pa < pb))

    def first(self, a, b):
        if b is None:
            return a
        if a is None:
            return b
        return jnp.maximum(a[0], b[0]), jnp.where(self.precedes(a, b), a[1], b[1])

    def exchange(self, a, b):
        if b is None:
            return a, None
        if a is None:
            return b, None
        c = self.precedes(a, b)
        first_payload = jnp.where(c, a[1], b[1])
        zero = self.tick()
        if zero is not None:
            first_payload = first_payload + zero
        return ((jnp.maximum(a[0], b[0]), first_payload),
                (jnp.minimum(a[0], b[0]), jnp.where(c, b[1], a[1])))

    def sort16(self, items):
        items = list(items)
        for i, j in _SORT16:
            items[i], items[j] = self.exchange(items[i], items[j])
        return items

    def merge16(self, xs, ys):
        n = PEER_TOPK
        xs = list(xs) + [None] * (n - len(xs))
        ys = list(ys) + [None] * (n - len(ys))
        h = [self.first(xs[i], ys[n - 1 - i]) for i in range(n)]
        d = n // 2
        while d:
            for i in range(n):
                if not i & d:
                    h[i], h[i + d] = self.exchange(h[i], h[i + d])
            d //= 2
        return [x for x in h if x is not None]

    def top16_of_keys(self, s_ref):
        runs = []
        for g in range(PEER_KEYS // PEER_TOPK):
            keys = range(g * PEER_TOPK, (g + 1) * PEER_TOPK)
            runs.append(self.sort16([(s_ref[pl.ds(k, PEER_HEADS, stride=PEER_KEYS), :], k) for k in keys]))
        while len(runs) > 1:
            runs = [self.merge16(runs[i], runs[i + 1]) for i in range(0, len(runs), 2)]
        return runs[0]


_EXPERT_BITS = 14
_EXCHANGES_PER_TOKEN = 12
_TOKENS_IN_FLIGHT = 12


def _scores(qp_ref, k1_ref, k2_ref, s1_scr, s2_scr):
    for h in range(PEER_HEADS):
        off = h * 2 * PEER_HALF
        rows = pl.ds(h * PEER_KEYS, PEER_KEYS)
        s1_scr[rows, :] = lax.dot_general(k1_ref[...], qp_ref[:, off:off + PEER_HALF], _NT,
                                          preferred_element_type=F32)
        s2_scr[rows, :] = lax.dot_general(k2_ref[...], qp_ref[:, off + PEER_HALF:off + 2 * PEER_HALF], _NT,
                                          preferred_element_type=F32)


def _retrieve(net, s1_scr, s2_scr):
    top1 = net.top16_of_keys(s1_scr)
    top2 = net.top16_of_keys(s2_scr)

    def pair(j, l):
        expert = top1[j][1] * PEER_KEYS + top2[l][1]
        return top1[j][0] + top2[l][0], expert + ((j * PEER_TOPK + l) << _EXPERT_BITS)

    half = PEER_TOPK // 2
    rows = [[pair(j, l) for l in range(PEER_TOPK // (j + 1))] for j in range(half)]
    col = [pair(j, 0) for j in range(half, PEER_TOPK)]
    merge = net.merge16
    small = merge(merge(rows[2], rows[3]), merge(merge(rows[4], rows[5]), merge(rows[6], rows[7])))
    top = merge(rows[0], merge(merge(rows[1], col), small))
    ex = [jnp.exp(v - top[0][0]) for v, _ in top]
    den = ex[0]
    for x in ex[1:]:
        den = den + x
    gates = jnp.concatenate([x / den for x in ex], axis=0)
    experts = jnp.concatenate([p & ((1 << _EXPERT_BITS) - 1) for _, p in top], axis=0)
    return gates.T, experts.T


def _gate_matrix(g, e, key):
    a = e >> (_EXPERT_BITS // 2)
    b = e & (PEER_KEYS - 1)
    sel_a = jnp.where(key == a, g, 0.0).astype(BF16)
    sel_b = jnp.where(key == b, 1.0, 0.0).astype(BF16)
    return lax.dot_general(sel_a, sel_b, _NT, preferred_element_type=F32)


def _select_kernel(qp_ref, k1_ref, k2_ref, w_ref, s1_scr, s2_scr, g_scr, e_scr):
    tm = qp_ref.shape[0]

    @pl.when(pl.program_id(0) == 0)
    def _():
        g_scr[...] = jnp.zeros_like(g_scr)
        e_scr[...] = jnp.zeros_like(e_scr)

    key = lax.broadcasted_iota(I32, (PEER_KEYS, g_scr.shape[1]), 0)
    done = []
    ticks = [0]

    def gate_matrix_of_next_token():
        t = len(done)
        w = _gate_matrix(g_scr[t:t + 1, :], e_scr[t:t + 1, :], key)
        w_ref[:, t * V7X_SUBLANES:(t + 1) * V7X_SUBLANES, :] = w.reshape(
            PEER_KEYS // V7X_SUBLANES, V7X_SUBLANES, PEER_KEYS)
        done.append(w)

    def tick():
        ticks[0] += 1
        if ticks[0] % _EXCHANGES_PER_TOKEN:
            return None
        if len(done) < tm:
            gate_matrix_of_next_token()
        waited = ticks[0] // _EXCHANGES_PER_TOKEN - 1 - _TOKENS_IN_FLIGHT
        if not 0 <= waited < len(done):
            return None
        bits = lax.bitcast_convert_type(done[waited][:V7X_SUBLANES], jnp.uint32)
        return lax.bitcast_convert_type((bits >> 16) >> 16, I32)

    _scores(qp_ref, k1_ref, k2_ref, s1_scr, s2_scr)
    gates, experts = _retrieve(_Network(tick), s1_scr, s2_scr)
    while len(done) < tm:
        gate_matrix_of_next_token()
    g_scr[...] = gates
    e_scr[...] = experts


def _peer_select(qp, k1, k2, tm):
    t = qp.shape[0]
    nt = t // tm
    nsel = PEER_HEADS * PEER_TOPK
    na = PEER_KEYS // V7X_SUBLANES
    keys = pl.BlockSpec((PEER_KEYS, PEER_HALF), lambda i: (0, 0))
    return pl.pallas_call(
        _select_kernel,
        grid=(nt + 1,),
        in_specs=[pl.BlockSpec((tm, qp.shape[1]), lambda i: (jnp.minimum(i, nt - 1), 0)), keys, keys],
        out_specs=pl.BlockSpec((na, tm * V7X_SUBLANES, PEER_KEYS), lambda i: (0, jnp.maximum(i - 1, 0), 0)),
        out_shape=jax.ShapeDtypeStruct((na, t * V7X_SUBLANES, PEER_KEYS), F32),
        scratch_shapes=[pltpu.VMEM((PEER_HEADS * PEER_KEYS, tm), F32),
                        pltpu.VMEM((PEER_HEADS * PEER_KEYS, tm), F32),
                        pltpu.VMEM((tm, nsel), F32), pltpu.VMEM((tm, nsel), I32)],
        compiler_params=_params("arbitrary"),
        name="peer_select",
    )(qp, k1, k2)


def _ffn_kernel(x_ref, u_ref, v_ref, w_ref, y_ref):
    tm = x_ref.shape[0]

    @pl.when(pl.program_id(1) == 0)
    def _():
        y_ref[...] = jnp.zeros_like(y_ref)

    g = lax.dot_general(x_ref[...], u_ref[...], _NT, preferred_element_type=F32)
    hs = []
    for al in range(V7X_SUBLANES):
        wa = w_ref[pl.ds(al, tm, stride=V7X_SUBLANES), :]
        hs.append((wa * _gelu(g[:, al * PEER_KEYS:(al + 1) * PEER_KEYS])).astype(BF16))
    h = jnp.concatenate(hs, axis=1)
    y_ref[...] += jnp.dot(h, v_ref[...], preferred_element_type=F32)


def _peer_ffn(xn, u, v, w, tm):
    t = xn.shape[0]
    ne = V7X_SUBLANES * PEER_KEYS
    return pl.pallas_call(
        _ffn_kernel,
        grid=(t // tm, u.shape[0] // ne),
        in_specs=[
            pl.BlockSpec((tm, D_MODEL), lambda i, n: (i, 0)),
            pl.BlockSpec((ne, D_MODEL), lambda i, n: (n, 0)),
            pl.BlockSpec((ne, D_MODEL), lambda i, n: (n, 0)),
            pl.BlockSpec((None, tm * V7X_SUBLANES, PEER_KEYS), lambda i, n: (n, i, 0)),
        ],
        out_specs=pl.BlockSpec((tm, D_MODEL), lambda i, n: (i, 0)),
        out_shape=jax.ShapeDtypeStruct((t, D_MODEL), F32),
        compiler_params=_params("parallel", "arbitrary"),
        name="peer_ffn",
    )(xn, u, v, w)


def _final_kernel(h_ref, y_ref, p_ref, g3_ref, wg_ref, wp_ref, gf_ref, o_ref):
    h = h_ref[...] + y_ref[...]
    xn = _rms_norm(h, g3_ref[...]).astype(BF16)
    gate = _sigmoid(jnp.dot(xn, wg_ref[...], preferred_element_type=F32))
    proj = jnp.dot(p_ref[...].astype(BF16), wp_ref[...], preferred_element_type=F32)
    o_ref[...] = _rms_norm(h + gate * proj, gf_ref[...])


def _final(h1, y, p2, g3, wg, wp, gf, tm):
    t = h1.shape[0]
    tok = lambda w: pl.BlockSpec((tm, w), lambda i: (i, 0))
    full = lambda a: pl.BlockSpec(a.shape, lambda i: (0, 0))
    return pl.pallas_call(
        _final_kernel,
        grid=(t // tm,),
        in_specs=[tok(D_MODEL), tok(D_MODEL), tok(PLE_DIM), full(g3), full(wg), full(wp), full(gf)],
        out_specs=tok(D_MODEL),
        out_shape=jax.ShapeDtypeStruct((t, D_MODEL), F32),
        compiler_params=_params("parallel"),
        name="ple_final",
    )(h1, y, p2, g3, wg, wp, gf)


_TM_PROJ = 512
_TM_SELECT = 128
_TM_FFN = 1024


def _double_heads(w):
    d = w.shape[0]
    w = w.reshape(d, N_KV_HEADS, 1, HEAD_DIM)
    return jnp.broadcast_to(w, (d, N_KV_HEADS, 2, HEAD_DIM)).reshape(d, 2 * KV_W)


def kernel(x, p, norm_mix_g, w_in, attn_sink, conv_w, conv_b, lru_wa, lru_ba, lru_wx, lru_bx, lru_lambda,
           w_up_attn, w_up_lru, w_o, norm_ffn_g, peer_wq, peer_k1, peer_k2, peer_u, peer_v, norm_ple_g,
           ple_w_gate, ple_w_proj, final_g):
    bsz, seq, d = x.shape
    t = bsz * seq
    assert p.shape[0] == 1, "single-layer block: the final norm follows layer 0"
    h = x.reshape(t, d)
    for l in range(1):
        row = lambda a: a.reshape(1, -1)
        wi = w_in[l]
        w_comb = jnp.concatenate(
            [wi[:, :Q_W], _double_heads(wi[:, Q_W:Q_W + KV_W]), _double_heads(wi[:, Q_W + KV_W:Q_W + 2 * KV_W]),
             wi[:, Q_W + 2 * KV_W:]], axis=1).astype(BF16)
        q, kd, vd, lx, lg, ga, gl, u_bf16 = _in_proj(h, row(norm_mix_g[l]), w_comb, peer_u[l], _TM_PROJ)
        attn = _attention(attn_sink[l], q.reshape(bsz, seq, -1), kd.reshape(bsz, seq, -1),
                          vd.reshape(bsz, seq, -1)).reshape(t, Q_W)
        rec = _rg_lru(lx.reshape(bsz, seq, d), lg.reshape(bsz, seq, d), conv_w[l], row(conv_b[l]),
                      lru_wa[l].astype(BF16), row(lru_ba[l]), lru_wx[l].astype(BF16), row(lru_bx[l]),
                      row(lru_lambda[l])).reshape(t, d)
        h1, xn2, qp, v_bf16 = _merge(attn, rec, ga, gl, h, w_up_attn[l].astype(BF16), w_up_lru[l].astype(BF16),
                                     w_o[l].astype(BF16), row(norm_ffn_g[l]), peer_wq[l].astype(BF16),
                                     peer_v[l], _TM_PROJ)
        w = _peer_select(qp, peer_k1[l].astype(BF16), peer_k2[l].astype(BF16), _TM_SELECT)
        y = _peer_ffn(xn2, u_bf16, v_bf16, w, _TM_FFN)
        h = _final(h1, y, p[l].reshape(t, PLE_DIM), row(norm_ple_g[l]), ple_w_gate[l].astype(BF16),
                   ple_w_proj[l].astype(BF16), row(final_g), _TM_PROJ)
    return h.reshape(bsz, seq, d)
```

```python
import jax
import jax.numpy as jnp
from jax import lax
from jax.experimental import pallas as pl
from jax.experimental.pallas import tpu as pltpu

D_MODEL = 1024
PLE_DIM = 256
EPS = 1e-6
NEG_INF = -1e30
N_Q_HEADS = 8
N_KV_HEADS = 2
HEAD_DIM = 64
WINDOW = 128
BLOCK = 128
LRU_BLOCKS = 8
LRU_BLOCK_DIM = 128
CONV_WIDTH = 4
LRU_C = 8.0
PEER_HEADS = 8
PEER_KEYS = 128
PEER_HALF = 128
PEER_TOPK = 16
Q_W = N_Q_HEADS * HEAD_DIM
KV_W = N_KV_HEADS * HEAD_DIM

V7X_LANES = 128
V7X_SUBLANES = 8
V7X_VMEM_LIMIT_BYTES = 56 * 1024 * 1024

F32 = jnp.float32
BF16 = jnp.bfloat16
I32 = jnp.int32

_NT = (((1,), (1,)), ((), ()))


def _gelu(x):
    return 0.5 * x * (1.0 + jnp.tanh(0.7978845608028654 * (x + 0.044715 * (x * x * x))))


def _sigmoid(x):
    return 0.5 * (jnp.tanh(0.5 * x) + 1.0)


def _rms_norm(xf, g):
    ms = jnp.mean(xf * xf, axis=-1, keepdims=True)
    return xf * lax.rsqrt(ms + EPS) * g


def _params(*sem):
    return pltpu.CompilerParams(dimension_semantics=sem, vmem_limit_bytes=V7X_VMEM_LIMIT_BYTES)


_IN_COLS = (Q_W, 2 * KV_W, 2 * KV_W, D_MODEL, D_MODEL, D_MODEL, D_MODEL)


def _in_proj_kernel(x_ref, g_ref, w_ref, table_ref, *out_refs):
    *proj_refs, table_bf16_ref = out_refs
    xn = _rms_norm(x_ref[...], g_ref[...]).astype(BF16)
    c0 = 0
    for ref, width in zip(proj_refs, _IN_COLS):
        ref[...] = jnp.dot(xn, w_ref[:, c0:c0 + width], preferred_element_type=F32).astype(ref.dtype)
        c0 += width
    table_bf16_ref[...] = table_ref[...].astype(BF16)


def _table_specs(table, steps):
    rows = table.shape[0] // steps
    spec = pl.BlockSpec((rows, table.shape[1]), lambda i: (i, 0))
    return spec, jax.ShapeDtypeStruct(table.shape, BF16)


def _in_proj(x2, g, w, table, tm):
    t = x2.shape[0]
    n_in = sum(_IN_COLS)
    table_spec, table_shape = _table_specs(table, t // tm)
    return pl.pallas_call(
        _in_proj_kernel,
        grid=(t // tm,),
        in_specs=[
            pl.BlockSpec((tm, D_MODEL), lambda i: (i, 0)),
            pl.BlockSpec((1, D_MODEL), lambda i: (0, 0)),
            pl.BlockSpec((D_MODEL, n_in), lambda i: (0, 0)),
            table_spec,
        ],
        out_specs=[pl.BlockSpec((tm, c), lambda i: (i, 0)) for c in _IN_COLS] + [table_spec],
        out_shape=[jax.ShapeDtypeStruct((t, c), BF16) for c in _IN_COLS] + [table_shape],
        compiler_params=_params("parallel"),
        name="in_proj",
    )(x2, g, w, table)


def _attn_kernel(sink_ref, q_ref, kc_ref, kp_ref, vc_ref, vp_ref, o_ref):
    n = pl.program_id(1)
    qi = lax.broadcasted_iota(I32, (BLOCK, 2 * BLOCK), 0)
    kj = lax.broadcasted_iota(I32, (BLOCK, 2 * BLOCK), 1)
    dist_i = qi + BLOCK - kj
    dist = dist_i.astype(F32)
    valid = (dist_i >= 0) & (dist_i < WINDOW) & ((n * BLOCK + kj - BLOCK) >= 0)
    kk = jnp.concatenate([kp_ref[...], kc_ref[...]], axis=0)
    vv = jnp.concatenate([vp_ref[...], vc_ref[...]], axis=0)
    lane = lax.broadcasted_iota(I32, (BLOCK, V7X_LANES), 1)
    upper = lane >= HEAD_DIM
    scale = HEAD_DIM ** -0.5
    for pair in range(N_Q_HEADS // 2):
        grp = (2 * pair) // (N_Q_HEADS // N_KV_HEADS)
        qp = q_ref[:, pair * V7X_LANES:(pair + 1) * V7X_LANES]
        kk_g = kk[:, grp * V7X_LANES:(grp + 1) * V7X_LANES]
        vv_g = vv[:, grp * V7X_LANES:(grp + 1) * V7X_LANES]
        outs = []
        for sub in range(2):
            head = 2 * pair + sub
            keep = upper if sub else jnp.logical_not(upper)
            qm = jnp.where(keep, qp, jnp.zeros_like(qp))
            s = lax.dot_general(qm, kk_g, _NT, preferred_element_type=F32)
            slope = 2.0 ** (-8.0 * (head + 1) / N_Q_HEADS)
            s = jnp.where(valid, s * scale - slope * dist, NEG_INF)
            sk = sink_ref[head]
            m = jnp.maximum(jnp.max(s, axis=-1, keepdims=True), sk)
            e = jnp.exp(s - m)
            den = jnp.sum(e, axis=-1, keepdims=True) + jnp.exp(sk - m)
            p = (e * (1.0 / den)).astype(BF16)
            outs.append(jnp.dot(p, vv_g, preferred_element_type=F32))
        o_pair = jnp.where(upper, outs[1], outs[0])
        o_ref[:, pair * V7X_LANES:(pair + 1) * V7X_LANES] = o_pair.astype(o_ref.dtype)


def _attention(sink, q3, k3, v3):
    b, s, _ = q3.shape
    nb = s // BLOCK
    cur = lambda bi, n: (bi, n, 0)
    prev = lambda bi, n: (bi, jnp.maximum(n - 1, 0), 0)
    return pl.pallas_call(
        _attn_kernel,
        grid=(b, nb),
        in_specs=[
            pl.BlockSpec(memory_space=pltpu.SMEM),
            pl.BlockSpec((None, BLOCK, Q_W), cur),
            pl.BlockSpec((None, BLOCK, 2 * KV_W), cur),
            pl.BlockSpec((None, BLOCK, 2 * KV_W), prev),
            pl.BlockSpec((None, BLOCK, 2 * KV_W), cur),
            pl.BlockSpec((None, BLOCK, 2 * KV_W), prev),
        ],
        out_specs=pl.BlockSpec((None, BLOCK, Q_W), cur),
        out_shape=jax.ShapeDtypeStruct((b, s, Q_W), BF16),
        compiler_params=_params("parallel", "parallel"),
        name="swa_attention",
    )(sink, q3, k3, k3, v3, v3)


def _lru_kernel(x_ref, gate_ref, cw_ref, cb_ref, wa_ref, ba_ref, wx_ref, bx_ref, lam_ref, o_ref):
    s_len = x_ref.shape[0]
    x = x_ref[...].astype(F32)
    row = lax.broadcasted_iota(I32, x.shape, 0)

    def shift(v, d, fill):
        return jnp.where(row >= d, pltpu.roll(v, d, axis=0), fill)

    cw = cw_ref[...]
    y = cb_ref[...] + shift(x, 3, 0.0) * cw[0:1]
    y = y + shift(x, 2, 0.0) * cw[1:2]
    y = y + shift(x, 1, 0.0) * cw[2:3]
    y = y + x * cw[3:4]
    yb = y.astype(BF16)
    r = _sigmoid(jnp.dot(yb, wa_ref[...], preferred_element_type=F32) + ba_ref[...])
    i = _sigmoid(jnp.dot(yb, wx_ref[...], preferred_element_type=F32) + bx_ref[...])
    z = -lam_ref[...]
    softplus = jnp.maximum(z, 0.0) + jnp.log1p(jnp.exp(-jnp.abs(z)))
    log_a = (-LRU_C) * r * softplus
    a = jnp.exp(log_a)
    z = 1.0 - a * a
    root = jnp.where(z > 0.0, z * lax.rsqrt(z), 0.0)
    b = root * (i * y)
    grp = V7X_SUBLANES
    a = a.reshape(s_len // grp, grp, -1)
    b = b.reshape(a.shape)
    sub = lax.broadcasted_iota(I32, a.shape, 1)
    d = 1
    while d < grp:
        inside = sub >= d
        b = a * jnp.where(inside, pltpu.roll(b, d, axis=1), 0.0) + b
        a = a * jnp.where(inside, pltpu.roll(a, d, axis=1), 1.0)
        d *= 2
    a = a.reshape(x.shape)
    b = b.reshape(x.shape)
    gate = _gelu(gate_ref[...].astype(F32))
    carry = jnp.zeros((1, x.shape[1]), F32)
    step = 2 * grp
    for t0 in range(0, s_len, step):
        hs = []
        for t in range(t0, t0 + step, grp):
            h = a[t:t + grp] * carry + b[t:t + grp]
            carry = h[grp - 1:grp]
            hs.append(h)
        o_ref[t0:t0 + step, :] = (jnp.concatenate(hs, axis=0) * gate[t0:t0 + step]).astype(o_ref.dtype)


def _rg_lru(lx3, lg3, conv_w, conv_b, wa, ba, wx, bx, lam):
    b, s, c = lx3.shape
    bd = LRU_BLOCK_DIM
    act = lambda bi, ci: (bi, 0, ci)
    vec = lambda bi, ci: (0, ci)
    mat = lambda bi, ci: (ci, 0, 0)
    return pl.pallas_call(
        _lru_kernel,
        grid=(b, c // bd),
        in_specs=[
            pl.BlockSpec((None, s, bd), act),
            pl.BlockSpec((None, s, bd), act),
            pl.BlockSpec((CONV_WIDTH, bd), vec),
            pl.BlockSpec((1, bd), vec),
            pl.BlockSpec((None, bd, bd), mat),
            pl.BlockSpec((1, bd), vec),
            pl.BlockSpec((None, bd, bd), mat),
            pl.BlockSpec((1, bd), vec),
            pl.BlockSpec((1, bd), vec),
        ],
        out_specs=pl.BlockSpec((None, s, bd), act),
        out_shape=jax.ShapeDtypeStruct((b, s, c), BF16),
        compiler_params=_params("parallel", "parallel"),
        name="rg_lru",
    )(lx3, lg3, conv_w, conv_b, wa, ba, wx, bx, lam)


def _merge_kernel(attn_ref, rec_ref, ga_ref, gl_ref, x_ref, wua_ref, wul_ref, wo_ref, g2_ref, wq_ref,
                  table_ref, h_ref, xn_ref, qp_ref, table_bf16_ref):
    table_bf16_ref[...] = table_ref[...].astype(BF16)
    up_a = jnp.dot(attn_ref[...], wua_ref[...], preferred_element_type=F32)
    up_l = jnp.dot(rec_ref[...], wul_ref[...], preferred_element_type=F32)
    merged = _sigmoid(ga_ref[...].astype(F32)) * up_a + _sigmoid(gl_ref[...].astype(F32)) * up_l
    h = x_ref[...] + jnp.dot(merged.astype(BF16), wo_ref[...], preferred_element_type=F32)
    h_ref[...] = h
    xn = _rms_norm(h, g2_ref[...]).astype(BF16)
    xn_ref[...] = xn
    qp_ref[...] = jnp.dot(xn, wq_ref[...], preferred_element_type=F32).astype(qp_ref.dtype)


def _merge(attn, rec, ga, gl, x2, wua, wul, wo, g2, wq, table, tm):
    t = x2.shape[0]
    nq = wq.shape[1]
    tok = lambda w: pl.BlockSpec((tm, w), lambda i: (i, 0))
    full = lambda a: pl.BlockSpec(a.shape, lambda i: (0, 0))
    table_spec, table_shape = _table_specs(table, t // tm)
    return pl.pallas_call(
        _merge_kernel,
        grid=(t // tm,),
        in_specs=[tok(Q_W), tok(D_MODEL), tok(D_MODEL), tok(D_MODEL), tok(D_MODEL),
                  full(wua), full(wul), full(wo), full(g2), full(wq), table_spec],
        out_specs=[tok(D_MODEL), tok(D_MODEL), tok(nq), table_spec],
        out_shape=[jax.ShapeDtypeStruct((t, D_MODEL), F32),
                   jax.ShapeDtypeStruct((t, D_MODEL), BF16),
                   jax.ShapeDtypeStruct((t, nq), BF16),
                   table_shape],
        compiler_params=_params("parallel"),
        name="merge_proj",
    )(attn, rec, ga, gl, x2, wua, wul, wo, g2, wq, table)


def _oddeven_merge(lo, hi, r):
    step = r * 2
    if step < hi - lo:
        yield from _oddeven_merge(lo, hi, step)
        yield from _oddeven_merge(lo + r, hi, step)
        yield from [(i, i + r) for i in range(lo + r, hi - r, step)]
    else:
        yield (lo, lo + r)


def _oddeven_merge_sort(lo, hi):
    if hi - lo >= 1:
        mid = lo + (hi - lo) // 2
        yield from _oddeven_merge_sort(lo, mid)
        yield from _oddeven_merge_sort(mid + 1, hi)
        yield from _oddeven_merge(lo, hi, 1)


_SORT16 = tuple(_oddeven_merge_sort(0, PEER_TOPK - 1))


def _no_tick():
    return None


def _zero_scalar(items):
    bits = lax.bitcast_convert_type(items[0][1], jnp.uint32)
    zeros = lax.bitcast_convert_type((bits >> 16) >> 16, F32)
    return jnp.max(zeros).astype(I32)


class _Network:
    def __init__(self, tick=_no_tick):
        self.tick = tick

    def precedes(self, a, b):
        va, pa = a
        vb, pb = b
        if isinstance(pa, int) and isinstance(pb, int):
            return (va >= vb) if pa < pb else (va > vb)
        return (va > vb) | ((va == vb) & (pa < pb))

    def first(self, a, b):
        if b is None:
            return a
        if a is None:
            return b
        return jnp.maximum(a[0], b[0]), jnp.where(self.precedes(a, b), a[1], b[1])

    def exchange(self, a, b):
        if b is None:
            return a, None
        if a is None:
            return b, None
        c = self.precedes(a, b)
        first_payload = jnp.where(c, a[1], b[1])
        zero = self.tick()
        if zero is not None:
            first_payload = first_payload + zero
        return ((jnp.maximum(a[0], b[0]), first_payload),
                (jnp.minimum(a[0], b[0]), jnp.where(c, b[1], a[1])))

    def sort16(self, items):
        items = list(items)
        for i, j in _SORT16:
            items[i], items[j] = self.exchange(items[i], items[j])
        return items

    def merge16(self, xs, ys):
        n = PEER_TOPK
        xs = list(xs) + [None] * (n - len(xs))
        ys = list(ys) + [None] * (n - len(ys))
        h = [self.first(xs[i], ys[n - 1 - i]) for i in range(n)]
        d = n // 2
        while d:
            for i in range(n):
                if not i & d:
                    h[i], h[i + d] = self.exchange(h[i], h[i + d])
            d //= 2
        return [x for x in h if x is not None]

    def top16_of_keys(self, s_ref, after=None):
        runs = []
        earlier = [after, after]
        for g in range(PEER_KEYS // PEER_TOPK):
            keys = range(g * PEER_TOPK, (g + 1) * PEER_TOPK)
            base = 0 if earlier[g] is None else _zero_scalar(earlier[g])
            earlier.append(self.sort16(
                [(s_ref[pl.ds(k + base, PEER_HEADS, stride=PEER_KEYS), :], k) for k in keys]))
            runs.append(earlier[-1])
        while len(runs) > 1:
            runs = [self.merge16(runs[i], runs[i + 1]) for i in range(0, len(runs), 2)]
        return runs[0]


_EXPERT_BITS = 14
_EXCHANGES_PER_TOKEN = 12
_TOKENS_IN_FLIGHT = 12


def _scores(qp_ref, k1_ref, k2_ref, s1_scr, s2_scr):
    for h in range(PEER_HEADS):
        off = h * 2 * PEER_HALF
        rows = pl.ds(h * PEER_KEYS, PEER_KEYS)
        s1_scr[rows, :] = lax.dot_general(k1_ref[...], qp_ref[:, off:off + PEER_HALF], _NT,
                                          preferred_element_type=F32)
        s2_scr[rows, :] = lax.dot_general(k2_ref[...], qp_ref[:, off + PEER_HALF:off + 2 * PEER_HALF], _NT,
                                          preferred_element_type=F32)


def _retrieve(net, s1_scr, s2_scr):
    top1 = net.top16_of_keys(s1_scr)
    top2 = net.top16_of_keys(s2_scr, after=top1)

    def pair(j, l):
        expert = top1[j][1] * PEER_KEYS + top2[l][1]
        return top1[j][0] + top2[l][0], expert + ((j * PEER_TOPK + l) << _EXPERT_BITS)

    half = PEER_TOPK // 2
    rows = [[pair(j, l) for l in range(PEER_TOPK // (j + 1))] for j in range(half)]
    col = [pair(j, 0) for j in range(half, PEER_TOPK)]
    merge = net.merge16
    small = merge(merge(rows[2], rows[3]), merge(merge(rows[4], rows[5]), merge(rows[6], rows[7])))
    top = merge(rows[0], merge(merge(rows[1], col), small))
    ex = [jnp.exp(v - top[0][0]) for v, _ in top]
    den = ex[0]
    for x in ex[1:]:
        den = den + x
    gates = jnp.concatenate([x / den for x in ex], axis=0)
    experts = jnp.concatenate([p & ((1 << _EXPERT_BITS) - 1) for _, p in top], axis=0)
    return gates.T, experts.T


def _gate_matrix(g, e, key):
    a = e >> (_EXPERT_BITS // 2)
    b = e & (PEER_KEYS - 1)
    sel_a = jnp.where(key == a, g, 0.0).astype(BF16)
    sel_b = jnp.where(key == b, 1.0, 0.0).astype(BF16)
    return lax.dot_general(sel_a, sel_b, _NT, preferred_element_type=F32)


def _select_kernel(qp_ref, k1_ref, k2_ref, w_ref, s1_scr, s2_scr, g_scr, e_scr):
    tm = qp_ref.shape[0]

    @pl.when(pl.program_id(0) == 0)
    def _():
        g_scr[...] = jnp.zeros_like(g_scr)
        e_scr[...] = jnp.zeros_like(e_scr)

    key = lax.broadcasted_iota(I32, (PEER_KEYS, g_scr.shape[1]), 0)
    done = []
    ticks = [0]

    def gate_matrix_of_next_token():
        t = len(done)
        w = _gate_matrix(g_scr[t:t + 1, :], e_scr[t:t + 1, :], key)
        w_ref[:, t * V7X_SUBLANES:(t + 1) * V7X_SUBLANES, :] = w.reshape(
            PEER_KEYS // V7X_SUBLANES, V7X_SUBLANES, PEER_KEYS)
        done.append(w)

    def tick():
        ticks[0] += 1
        if ticks[0] % _EXCHANGES_PER_TOKEN:
            return None
        if len(done) < tm:
            gate_matrix_of_next_token()
        waited = ticks[0] // _EXCHANGES_PER_TOKEN - 1 - _TOKENS_IN_FLIGHT
        if not 0 <= waited < len(done):
            return None
        bits = lax.bitcast_convert_type(done[waited][:V7X_SUBLANES], jnp.uint32)
        return lax.bitcast_convert_type((bits >> 16) >> 16, I32)

    _scores(qp_ref, k1_ref, k2_ref, s1_scr, s2_scr)
    gates, experts = _retrieve(_Network(tick), s1_scr, s2_scr)
    while len(done) < tm:
        gate_matrix_of_next_token()
    g_scr[...] = gates
    e_scr[...] = experts


def _peer_select(qp, k1, k2, tm):
    t = qp.shape[0]
    nt = t // tm
    nsel = PEER_HEADS * PEER_TOPK
    na = PEER_KEYS // V7X_SUBLANES
    keys = pl.BlockSpec((PEER_KEYS, PEER_HALF), lambda i: (0, 0))
    return pl.pallas_call(
        _select_kernel,
        grid=(nt + 1,),
        in_specs=[pl.BlockSpec((tm, qp.shape[1]), lambda i: (jnp.minimum(i, nt - 1), 0)), keys, keys],
        out_specs=pl.BlockSpec((na, tm * V7X_SUBLANES, PEER_KEYS), lambda i: (0, jnp.maximum(i - 1, 0), 0)),
        out_shape=jax.ShapeDtypeStruct((na, t * V7X_SUBLANES, PEER_KEYS), F32),
        scratch_shapes=[pltpu.VMEM((PEER_HEADS * PEER_KEYS, tm), F32),
                        pltpu.VMEM((PEER_HEADS * PEER_KEYS, tm), F32),
                        pltpu.VMEM((tm, nsel), F32), pltpu.VMEM((tm, nsel), I32)],
        compiler_params=_params("arbitrary"),
        name="peer_select",
    )(qp, k1, k2)


def _ffn_kernel(x_ref, u_ref, v_ref, w_ref, y_ref):
    tm = x_ref.shape[0]

    @pl.when(pl.program_id(1) == 0)
    def _():
        y_ref[...] = jnp.zeros_like(y_ref)

    g = lax.dot_general(x_ref[...], u_ref[...], _NT, preferred_element_type=F32)
    hs = []
    for al in range(V7X_SUBLANES):
        wa = w_ref[pl.ds(al, tm, stride=V7X_SUBLANES), :]
        hs.append((wa * _gelu(g[:, al * PEER_KEYS:(al + 1) * PEER_KEYS])).astype(BF16))
    h = jnp.concatenate(hs, axis=1)
    y_ref[...] += jnp.dot(h, v_ref[...], preferred_element_type=F32)


def _peer_ffn(xn, u, v, w, tm):
    t = xn.shape[0]
    ne = V7X_SUBLANES * PEER_KEYS
    return pl.pallas_call(
        _ffn_kernel,
        grid=(t // tm, u.shape[0] // ne),
        in_specs=[
            pl.BlockSpec((tm, D_MODEL), lambda i, n: (i, 0)),
            pl.BlockSpec((ne, D_MODEL), lambda i, n: (n, 0)),
            pl.BlockSpec((ne, D_MODEL), lambda i, n: (n, 0)),
            pl.BlockSpec((None, tm * V7X_SUBLANES, PEER_KEYS), lambda i, n: (n, i, 0)),
        ],
        out_specs=pl.BlockSpec((tm, D_MODEL), lambda i, n: (i, 0)),
        out_shape=jax.ShapeDtypeStruct((t, D_MODEL), F32),
        compiler_params=_params("parallel", "arbitrary"),
        name="peer_ffn",
    )(xn, u, v, w)


def _final_kernel(h_ref, y_ref, p_ref, g3_ref, wg_ref, wp_ref, gf_ref, o_ref):
    h = h_ref[...] + y_ref[...]
    xn = _rms_norm(h, g3_ref[...]).astype(BF16)
    gate = _sigmoid(jnp.dot(xn, wg_ref[...], preferred_element_type=F32))
    proj = jnp.dot(p_ref[...].astype(BF16), wp_ref[...], preferred_element_type=F32)
    o_ref[...] = _rms_norm(h + gate * proj, gf_ref[...])


def _final(h1, y, p2, g3, wg, wp, gf, tm):
    t = h1.shape[0]
    tok = lambda w: pl.BlockSpec((tm, w), lambda i: (i, 0))
    full = lambda a: pl.BlockSpec(a.shape, lambda i: (0, 0))
    return pl.pallas_call(
        _final_kernel,
        grid=(t // tm,),
        in_specs=[tok(D_MODEL), tok(D_MODEL), tok(PLE_DIM), full(g3), full(wg), full(wp), full(gf)],
        out_specs=tok(D_MODEL),
        out_shape=jax.ShapeDtypeStruct((t, D_MODEL), F32),
        compiler_params=_params("parallel"),
        name="ple_final",
    )(h1, y, p2, g3, wg, wp, gf)


_TM_PROJ = 512
_TM_SELECT = 128
_TM_FFN = 1024


def _double_heads(w):
    d = w.shape[0]
    w = w.reshape(d, N_KV_HEADS, 1, HEAD_DIM)
    return jnp.broadcast_to(w, (d, N_KV_HEADS, 2, HEAD_DIM)).reshape(d, 2 * KV_W)


def kernel(x, p, norm_mix_g, w_in, attn_sink, conv_w, conv_b, lru_wa, lru_ba, lru_wx, lru_bx, lru_lambda,
           w_up_attn, w_up_lru, w_o, norm_ffn_g, peer_wq, peer_k1, peer_k2, peer_u, peer_v, norm_ple_g,
           ple_w_gate, ple_w_proj, final_g):
    bsz, seq, d = x.shape
    t = bsz * seq
    assert p.shape[0] == 1, "single-layer block: the final norm follows layer 0"
    h = x.reshape(t, d)
    for l in range(1):
        row = lambda a: a.reshape(1, -1)
        wi = w_in[l]
        w_comb = jnp.concatenate(
            [wi[:, :Q_W], _double_heads(wi[:, Q_W:Q_W + KV_W]), _double_heads(wi[:, Q_W + KV_W:Q_W + 2 * KV_W]),
             wi[:, Q_W + 2 * KV_W:]], axis=1).astype(BF16)
        q, kd, vd, lx, lg, ga, gl, u_bf16 = _in_proj(h, row(norm_mix_g[l]), w_comb, peer_u[l], _TM_PROJ)
        attn = _attention(attn_sink[l], q.reshape(bsz, seq, -1), kd.reshape(bsz, seq, -1),
                          vd.reshape(bsz, seq, -1)).reshape(t, Q_W)
        rec = _rg_lru(lx.reshape(bsz, seq, d), lg.reshape(bsz, seq, d), conv_w[l], row(conv_b[l]),
                      lru_wa[l].astype(BF16), row(lru_ba[l]), lru_wx[l].astype(BF16), row(lru_bx[l]),
                      row(lru_lambda[l])).reshape(t, d)
        h1, xn2, qp, v_bf16 = _merge(attn, rec, ga, gl, h, w_up_attn[l].astype(BF16), w_up_lru[l].astype(BF16),
                                     w_o[l].astype(BF16), row(norm_ffn_g[l]), peer_wq[l].astype(BF16),
                                     peer_v[l], _TM_PROJ)
        w = _peer_select(qp, peer_k1[l].astype(BF16), peer_k2[l].astype(BF16), _TM_SELECT)
        y = _peer_ffn(xn2, u_bf16, v_bf16, w, _TM_FFN)
        h = _final(h1, y, p[l].reshape(t, PLE_DIM), row(norm_ple_g[l]), ple_w_gate[l].astype(BF16),
                   ple_w_proj[l].astype(BF16), row(final_g), _TM_PROJ)
    return h.reshape(bsz, seq, d)
```

```python
import jax
import jax.numpy as jnp
from jax import lax
from jax.experimental import pallas as pl
from jax.experimental.pallas import tpu as pltpu

D_MODEL = 1024
PLE_DIM = 256
EPS = 1e-6
NEG_INF = -1e30
N_Q_HEADS = 8
N_KV_HEADS = 2
HEAD_DIM = 64
WINDOW = 128
BLOCK = 128
LRU_BLOCKS = 8
LRU_BLOCK_DIM = 128
CONV_WIDTH = 4
LRU_C = 8.0
PEER_HEADS = 8
PEER_KEYS = 128
PEER_HALF = 128
PEER_TOPK = 16
Q_W = N_Q_HEADS * HEAD_DIM
KV_W = N_KV_HEADS * HEAD_DIM

V7X_LANES = 128
V7X_SUBLANES = 8
V7X_VMEM_LIMIT_BYTES = 56 * 1024 * 1024

F32 = jnp.float32
BF16 = jnp.bfloat16
I32 = jnp.int32

_NT = (((1,), (1,)), ((), ()))


def _gelu(x):
    return 0.5 * x * (1.0 + jnp.tanh(0.7978845608028654 * (x + 0.044715 * (x * x * x))))


def _sigmoid(x):
    return 0.5 * (jnp.tanh(0.5 * x) + 1.0)


def _rms_norm(xf, g):
    ms = jnp.mean(xf * xf, axis=-1, keepdims=True)
    return xf * lax.rsqrt(ms + EPS) * g


def _params(*sem):
    return pltpu.CompilerParams(dimension_semantics=sem, vmem_limit_bytes=V7X_VMEM_LIMIT_BYTES)


_IN_COLS = (Q_W, 2 * KV_W, 2 * KV_W, D_MODEL, D_MODEL, D_MODEL, D_MODEL)


def _in_proj_kernel(x_ref, g_ref, w_ref, table_ref, *out_refs):
    *proj_refs, table_bf16_ref = out_refs
    xn = _rms_norm(x_ref[...], g_ref[...]).astype(BF16)
    c0 = 0
    for ref, width in zip(proj_refs, _IN_COLS):
        ref[...] = jnp.dot(xn, w_ref[:, c0:c0 + width], preferred_element_type=F32).astype(ref.dtype)
        c0 += width
    table_bf16_ref[...] = table_ref[...].astype(BF16)


def _table_specs(table, steps):
    rows = table.shape[0] // steps
    spec = pl.BlockSpec((rows, table.shape[1]), lambda i: (i, 0))
    return spec, jax.ShapeDtypeStruct(table.shape, BF16)


def _in_proj(x2, g, w, table, tm):
    t = x2.shape[0]
    n_in = sum(_IN_COLS)
    table_spec, table_shape = _table_specs(table, t // tm)
    return pl.pallas_call(
        _in_proj_kernel,
        grid=(t // tm,),
        in_specs=[
            pl.BlockSpec((tm, D_MODEL), lambda i: (i, 0)),
            pl.BlockSpec((1, D_MODEL), lambda i: (0, 0)),
            pl.BlockSpec((D_MODEL, n_in), lambda i: (0, 0)),
            table_spec,
        ],
        out_specs=[pl.BlockSpec((tm, c), lambda i: (i, 0)) for c in _IN_COLS] + [table_spec],
        out_shape=[jax.ShapeDtypeStruct((t, c), BF16) for c in _IN_COLS] + [table_shape],
        compiler_params=_params("parallel"),
        name="in_proj",
    )(x2, g, w, table)


_ATTN_BLOCKS = 8


def _attn_kernel(sink_ref, q_ref, kc_ref, kp_ref, vc_ref, vp_ref, o_ref):
    n = pl.program_id(1)
    qi = lax.broadcasted_iota(I32, (BLOCK, 2 * BLOCK), 0)
    kj = lax.broadcasted_iota(I32, (BLOCK, 2 * BLOCK), 1)
    dist_i = qi + BLOCK - kj
    dist = dist_i.astype(F32)
    in_window = (dist_i >= 0) & (dist_i < WINDOW)
    lane = lax.broadcasted_iota(I32, (BLOCK, V7X_LANES), 1)
    upper = lane >= HEAD_DIM
    scale = HEAD_DIM ** -0.5
    for j in range(_ATTN_BLOCKS):
        rows = slice(j * BLOCK, (j + 1) * BLOCK)
        before = slice((j - 1) * BLOCK, j * BLOCK)
        k_prev = kc_ref[before, :] if j else kp_ref[...]
        v_prev = vc_ref[before, :] if j else vp_ref[...]
        kk = jnp.concatenate([k_prev, kc_ref[rows, :]], axis=0)
        vv = jnp.concatenate([v_prev, vc_ref[rows, :]], axis=0)
        valid = in_window & (((n * _ATTN_BLOCKS + j) * BLOCK + kj - BLOCK) >= 0)
        for pair in range(N_Q_HEADS // 2):
            grp = (2 * pair) // (N_Q_HEADS // N_KV_HEADS)
            qp = q_ref[rows, pair * V7X_LANES:(pair + 1) * V7X_LANES]
            kk_g = kk[:, grp * V7X_LANES:(grp + 1) * V7X_LANES]
            vv_g = vv[:, grp * V7X_LANES:(grp + 1) * V7X_LANES]
            outs = []
            for sub in range(2):
                head = 2 * pair + sub
                keep = upper if sub else jnp.logical_not(upper)
                qm = jnp.where(keep, qp, jnp.zeros_like(qp))
                s = lax.dot_general(qm, kk_g, _NT, preferred_element_type=F32)
                slope = 2.0 ** (-8.0 * (head + 1) / N_Q_HEADS)
                s = jnp.where(valid, s * scale - slope * dist, NEG_INF)
                sk = sink_ref[head]
                m = jnp.maximum(jnp.max(s, axis=-1, keepdims=True), sk)
                e = jnp.exp(s - m)
                den = jnp.sum(e, axis=-1, keepdims=True) + jnp.exp(sk - m)
                p = (e * (1.0 / den)).astype(BF16)
                outs.append(jnp.dot(p, vv_g, preferred_element_type=F32))
            o_pair = jnp.where(upper, outs[1], outs[0])
            o_ref[rows, pair * V7X_LANES:(pair + 1) * V7X_LANES] = o_pair.astype(o_ref.dtype)


def _attention(sink, q3, k3, v3):
    b, s, _ = q3.shape
    rows = _ATTN_BLOCKS * BLOCK
    cur = lambda bi, n: (bi, n, 0)
    prev = lambda bi, n: (bi, jnp.maximum(n * _ATTN_BLOCKS - 1, 0), 0)
    return pl.pallas_call(
        _attn_kernel,
        grid=(b, s // rows),
        in_specs=[
            pl.BlockSpec(memory_space=pltpu.SMEM),
            pl.BlockSpec((None, rows, Q_W), cur),
            pl.BlockSpec((None, rows, 2 * KV_W), cur),
            pl.BlockSpec((None, BLOCK, 2 * KV_W), prev),
            pl.BlockSpec((None, rows, 2 * KV_W), cur),
            pl.BlockSpec((None, BLOCK, 2 * KV_W), prev),
        ],
        out_specs=pl.BlockSpec((None, rows, Q_W), cur),
        out_shape=jax.ShapeDtypeStruct((b, s, Q_W), BF16),
        compiler_params=_params("parallel", "parallel"),
        name="swa_attention",
    )(sink, q3, k3, k3, v3, v3)


def _lru_kernel(x_ref, gate_ref, cw_ref, cb_ref, wa_ref, ba_ref, wx_ref, bx_ref, lam_ref, o_ref):
    s_len = x_ref.shape[0]
    x = x_ref[...].astype(F32)
    row = lax.broadcasted_iota(I32, x.shape, 0)

    def shift(v, d, fill):
        return jnp.where(row >= d, pltpu.roll(v, d, axis=0), fill)

    cw = cw_ref[...]
    y = cb_ref[...] + shift(x, 3, 0.0) * cw[0:1]
    y = y + shift(x, 2, 0.0) * cw[1:2]
    y = y + shift(x, 1, 0.0) * cw[2:3]
    y = y + x * cw[3:4]
    yb = y.astype(BF16)
    r = _sigmoid(jnp.dot(yb, wa_ref[...], preferred_element_type=F32) + ba_ref[...])
    i = _sigmoid(jnp.dot(yb, wx_ref[...], preferred_element_type=F32) + bx_ref[...])
    z = -lam_ref[...]
    softplus = jnp.maximum(z, 0.0) + jnp.log1p(jnp.exp(-jnp.abs(z)))
    log_a = (-LRU_C) * r * softplus
    a = jnp.exp(log_a)
    z = 1.0 - a * a
    root = jnp.where(z > 0.0, z * lax.rsqrt(z), 0.0)
    b = root * (i * y)
    grp = V7X_SUBLANES
    a = a.reshape(s_len // grp, grp, -1)
    b = b.reshape(a.shape)
    sub = lax.broadcasted_iota(I32, a.shape, 1)
    d = 1
    while d < grp:
        inside = sub >= d
        b = a * jnp.where(inside, pltpu.roll(b, d, axis=1), 0.0) + b
        a = a * jnp.where(inside, pltpu.roll(a, d, axis=1), 1.0)
        d *= 2
    a = a.reshape(x.shape)
    b = b.reshape(x.shape)
    gate = _gelu(gate_ref[...].astype(F32))
    carry = jnp.zeros((1, x.shape[1]), F32)
    step = 2 * grp
    for t0 in range(0, s_len, step):
        hs = []
        for t in range(t0, t0 + step, grp):
            h = a[t:t + grp] * carry + b[t:t + grp]
            carry = h[grp - 1:grp]
            hs.append(h)
        o_ref[t0:t0 + step, :] = (jnp.concatenate(hs, axis=0) * gate[t0:t0 + step]).astype(o_ref.dtype)


def _rg_lru(lx3, lg3, conv_w, conv_b, wa, ba, wx, bx, lam):
    b, s, c = lx3.shape
    bd = LRU_BLOCK_DIM
    act = lambda bi, ci: (bi, 0, ci)
    vec = lambda bi, ci: (0, ci)
    mat = lambda bi, ci: (ci, 0, 0)
    return pl.pallas_call(
        _lru_kernel,
        grid=(b, c // bd),
        in_specs=[
            pl.BlockSpec((None, s, bd), act),
            pl.BlockSpec((None, s, bd), act),
            pl.BlockSpec((CONV_WIDTH, bd), vec),
            pl.BlockSpec((1, bd), vec),
            pl.BlockSpec((None, bd, bd), mat),
            pl.BlockSpec((1, bd), vec),
            pl.BlockSpec((None, bd, bd), mat),
            pl.BlockSpec((1, bd), vec),
            pl.BlockSpec((1, bd), vec),
        ],
        out_specs=pl.BlockSpec((None, s, bd), act),
        out_shape=jax.ShapeDtypeStruct((b, s, c), BF16),
        compiler_params=_params("parallel", "parallel"),
        name="rg_lru",
    )(lx3, lg3, conv_w, conv_b, wa, ba, wx, bx, lam)


def _merge_kernel(attn_ref, rec_ref, ga_ref, gl_ref, x_ref, wua_ref, wul_ref, wo_ref, g2_ref, wq_ref,
                  table_ref, h_ref, xn_ref, qp_ref, table_bf16_ref):
    table_bf16_ref[...] = table_ref[...].astype(BF16)
    up_a = jnp.dot(attn_ref[...], wua_ref[...], preferred_element_type=F32)
    up_l = jnp.dot(rec_ref[...], wul_ref[...], preferred_element_type=F32)
    merged = _sigmoid(ga_ref[...].astype(F32)) * up_a + _sigmoid(gl_ref[...].astype(F32)) * up_l
    h = x_ref[...] + jnp.dot(merged.astype(BF16), wo_ref[...], preferred_element_type=F32)
    h_ref[...] = h
    xn = _rms_norm(h, g2_ref[...]).astype(BF16)
    xn_ref[...] = xn
    qp_ref[...] = jnp.dot(xn, wq_ref[...], preferred_element_type=F32).astype(qp_ref.dtype)


def _merge(attn, rec, ga, gl, x2, wua, wul, wo, g2, wq, table, tm):
    t = x2.shape[0]
    nq = wq.shape[1]
    tok = lambda w: pl.BlockSpec((tm, w), lambda i: (i, 0))
    full = lambda a: pl.BlockSpec(a.shape, lambda i: (0, 0))
    table_spec, table_shape = _table_specs(table, t // tm)
    return pl.pallas_call(
        _merge_kernel,
        grid=(t // tm,),
        in_specs=[tok(Q_W), tok(D_MODEL), tok(D_MODEL), tok(D_MODEL), tok(D_MODEL),
                  full(wua), full(wul), full(wo), full(g2), full(wq), table_spec],
        out_specs=[tok(D_MODEL), tok(D_MODEL), tok(nq), table_spec],
        out_shape=[jax.ShapeDtypeStruct((t, D_MODEL), F32),
                   jax.ShapeDtypeStruct((t, D_MODEL), BF16),
                   jax.ShapeDtypeStruct((t, nq), BF16),
                   table_shape],
        compiler_params=_params("parallel"),
        name="merge_proj",
    )(attn, rec, ga, gl, x2, wua, wul, wo, g2, wq, table)


def _oddeven_merge(lo, hi, r):
    step = r * 2
    if step < hi - lo:
        yield from _oddeven_merge(lo, hi, step)
        yield from _oddeven_merge(lo + r, hi, step)
        yield from [(i, i + r) for i in range(lo + r, hi - r, step)]
    else:
        yield (lo, lo + r)


def _oddeven_merge_sort(lo, hi):
    if hi - lo >= 1:
        mid = lo + (hi - lo) // 2
        yield from _oddeven_merge_sort(lo, mid)
        yield from _oddeven_merge_sort(mid + 1, hi)
        yield from _oddeven_merge(lo, hi, 1)


_SORT16 = tuple(_oddeven_merge_sort(0, PEER_TOPK - 1))
_GROUPS_AHEAD = 2


def _no_tick():
    return None


def _zero_scalar(items):
    bits = lax.bitcast_convert_type(items[0][1], jnp.uint32)
    zeros = lax.bitcast_convert_type((bits >> 16) >> 16, F32)
    return jnp.max(zeros).astype(I32)


class _Network:
    def __init__(self, tick=_no_tick):
        self.tick = tick

    def precedes(self, a, b):
        va, pa = a
        vb, pb = b
        if isinstance(pa, int) and isinstance(pb, int):
            return (va >= vb) if pa < pb else (va > vb)
        return (va > vb) | ((va == vb) & (pa < pb))

    def first(self, a, b):
        if b is None:
            return a
        if a is None:
            return b
        return jnp.maximum(a[0], b[0]), jnp.where(self.precedes(a, b), a[1], b[1])

    def exchange(self, a, b):
        if b is None:
            return a, None
        if a is None:
            return b, None
        c = self.precedes(a, b)
        first_payload = jnp.where(c, a[1], b[1])
        zero = self.tick()
        if zero is not None:
            first_payload = first_payload + zero
        return ((jnp.maximum(a[0], b[0]), first_payload),
                (jnp.minimum(a[0], b[0]), jnp.where(c, b[1], a[1])))

    def sort16(self, items):
        items = list(items)
        for i, j in _SORT16:
            items[i], items[j] = self.exchange(items[i], items[j])
        return items

    def merge16(self, xs, ys):
        n = PEER_TOPK
        xs = list(xs) + [None] * (n - len(xs))
        ys = list(ys) + [None] * (n - len(ys))
        h = [self.first(xs[i], ys[n - 1 - i]) for i in range(n)]
        d = n // 2
        while d:
            for i in range(n):
                if not i & d:
                    h[i], h[i + d] = self.exchange(h[i], h[i + d])
            d //= 2
        return [x for x in h if x is not None]

    def top16_of_keys(self, s_ref, after=None):
        runs = []
        earlier = [after] * _GROUPS_AHEAD
        for g in range(PEER_KEYS // PEER_TOPK):
            keys = range(g * PEER_TOPK, (g + 1) * PEER_TOPK)
            base = 0 if earlier[g] is None else _zero_scalar(earlier[g])
            earlier.append(self.sort16(
                [(s_ref[pl.ds(k + base, PEER_HEADS, stride=PEER_KEYS), :], k) for k in keys]))
            runs.append(earlier[-1])
        while len(runs) > 1:
            runs = [self.merge16(runs[i], runs[i + 1]) for i in range(0, len(runs), 2)]
        return runs[0]


_EXPERT_BITS = 14
_EXCHANGES_PER_TOKEN = 12
_TOKENS_IN_FLIGHT = 12


def _scores(qp_ref, k1_ref, k2_ref, s1_scr, s2_scr):
    for h in range(PEER_HEADS):
        off = h * 2 * PEER_HALF
        rows = pl.ds(h * PEER_KEYS, PEER_KEYS)
        s1_scr[rows, :] = lax.dot_general(k1_ref[...], qp_ref[:, off:off + PEER_HALF], _NT,
                                          preferred_element_type=F32)
        s2_scr[rows, :] = lax.dot_general(k2_ref[...], qp_ref[:, off + PEER_HALF:off + 2 * PEER_HALF], _NT,
                                          preferred_element_type=F32)


def _retrieve(net, s1_scr, s2_scr):
    top1 = net.top16_of_keys(s1_scr)
    top2 = net.top16_of_keys(s2_scr, after=top1)

    def pair(j, l):
        expert = top1[j][1] * PEER_KEYS + top2[l][1]
        return top1[j][0] + top2[l][0], expert + ((j * PEER_TOPK + l) << _EXPERT_BITS)

    half = PEER_TOPK // 2
    rows = [[pair(j, l) for l in range(PEER_TOPK // (j + 1))] for j in range(half)]
    col = [pair(j, 0) for j in range(half, PEER_TOPK)]
    merge = net.merge16
    small = merge(merge(rows[2], rows[3]), merge(merge(rows[4], rows[5]), merge(rows[6], rows[7])))
    top = merge(rows[0], merge(merge(rows[1], col), small))
    ex = [jnp.exp(v - top[0][0]) for v, _ in top]
    den = ex[0]
    for x in ex[1:]:
        den = den + x
    gates = jnp.concatenate([x / den for x in ex], axis=0)
    experts = jnp.concatenate([p & ((1 << _EXPERT_BITS) - 1) for _, p in top], axis=0)
    return gates.T, experts.T


def _gate_matrix(g, e, key):
    a = e >> (_EXPERT_BITS // 2)
    b = e & (PEER_KEYS - 1)
    sel_a = jnp.where(key == a, g, 0.0).astype(BF16)
    sel_b = jnp.where(key == b, 1.0, 0.0).astype(BF16)
    return lax.dot_general(sel_a, sel_b, _NT, preferred_element_type=F32)


def _select_kernel(qp_ref, k1_ref, k2_ref, w_ref, s1_scr, s2_scr, g_scr, e_scr):
    tm = qp_ref.shape[0]

    @pl.when(pl.program_id(0) == 0)
    def _():
        g_scr[...] = jnp.zeros_like(g_scr)
        e_scr[...] = jnp.zeros_like(e_scr)

    key = lax.broadcasted_iota(I32, (PEER_KEYS, g_scr.shape[1]), 0)
    done = []
    ticks = [0]

    def gate_matrix_of_next_token():
        t = len(done)
        w = _gate_matrix(g_scr[t:t + 1, :], e_scr[t:t + 1, :], key)
        w_ref[:, t * V7X_SUBLANES:(t + 1) * V7X_SUBLANES, :] = w.reshape(
            PEER_KEYS // V7X_SUBLANES, V7X_SUBLANES, PEER_KEYS)
        done.append(w)

    def tick():
        ticks[0] += 1
        if ticks[0] % _EXCHANGES_PER_TOKEN:
            return None
        if len(done) < tm:
            gate_matrix_of_next_token()
        waited = ticks[0] // _EXCHANGES_PER_TOKEN - 1 - _TOKENS_IN_FLIGHT
        if not 0 <= waited < len(done):
            return None
        bits = lax.bitcast_convert_type(done[waited][:V7X_SUBLANES], jnp.uint32)
        return lax.bitcast_convert_type((bits >> 16) >> 16, I32)

    _scores(qp_ref, k1_ref, k2_ref, s1_scr, s2_scr)
    gates, experts = _retrieve(_Network(tick), s1_scr, s2_scr)
    while len(done) < tm:
        gate_matrix_of_next_token()
    g_scr[...] = gates
    e_scr[...] = experts


def _peer_select(qp, k1, k2, tm):
    t = qp.shape[0]
    nt = t // tm
    nsel = PEER_HEADS * PEER_TOPK
    na = PEER_KEYS // V7X_SUBLANES
    keys = pl.BlockSpec((PEER_KEYS, PEER_HALF), lambda i: (0, 0))
    return pl.pallas_call(
        _select_kernel,
        grid=(nt + 1,),
        in_specs=[pl.BlockSpec((tm, qp.shape[1]), lambda i: (jnp.minimum(i, nt - 1), 0)), keys, keys],
        out_specs=pl.BlockSpec((na, tm * V7X_SUBLANES, PEER_KEYS), lambda i: (0, jnp.maximum(i - 1, 0), 0)),
        out_shape=jax.ShapeDtypeStruct((na, t * V7X_SUBLANES, PEER_KEYS), F32),
        scratch_shapes=[pltpu.VMEM((PEER_HEADS * PEER_KEYS, tm), F32),
                        pltpu.VMEM((PEER_HEADS * PEER_KEYS, tm), F32),
                        pltpu.VMEM((tm, nsel), F32), pltpu.VMEM((tm, nsel), I32)],
        compiler_params=_params("arbitrary"),
        name="peer_select",
    )(qp, k1, k2)


def _ffn_kernel(x_ref, u_ref, v_ref, w_ref, y_ref):
    tm = x_ref.shape[0]

    @pl.when(pl.program_id(1) == 0)
    def _():
        y_ref[...] = jnp.zeros_like(y_ref)

    g = lax.dot_general(x_ref[...], u_ref[...], _NT, preferred_element_type=F32)
    hs = []
    for al in range(V7X_SUBLANES):
        wa = w_ref[pl.ds(al, tm, stride=V7X_SUBLANES), :]
        hs.append((wa * _gelu(g[:, al * PEER_KEYS:(al + 1) * PEER_KEYS])).astype(BF16))
    h = jnp.concatenate(hs, axis=1)
    y_ref[...] += jnp.dot(h, v_ref[...], preferred_element_type=F32)


def _peer_ffn(xn, u, v, w, tm):
    t = xn.shape[0]
    ne = V7X_SUBLANES * PEER_KEYS
    return pl.pallas_call(
        _ffn_kernel,
        grid=(t // tm, u.shape[0] // ne),
        in_specs=[
            pl.BlockSpec((tm, D_MODEL), lambda i, n: (i, 0)),
            pl.BlockSpec((ne, D_MODEL), lambda i, n: (n, 0)),
            pl.BlockSpec((ne, D_MODEL), lambda i, n: (n, 0)),
            pl.BlockSpec((None, tm * V7X_SUBLANES, PEER_KEYS), lambda i, n: (n, i, 0)),
        ],
        out_specs=pl.BlockSpec((tm, D_MODEL), lambda i, n: (i, 0)),
        out_shape=jax.ShapeDtypeStruct((t, D_MODEL), F32),
        compiler_params=_params("parallel", "arbitrary"),
        name="peer_ffn",
    )(xn, u, v, w)


def _final_kernel(h_ref, y_ref, p_ref, g3_ref, wg_ref, wp_ref, gf_ref, o_ref):
    h = h_ref[...] + y_ref[...]
    xn = _rms_norm(h, g3_ref[...]).astype(BF16)
    gate = _sigmoid(jnp.dot(xn, wg_ref[...], preferred_element_type=F32))
    proj = jnp.dot(p_ref[...].astype(BF16), wp_ref[...], preferred_element_type=F32)
    o_ref[...] = _rms_norm(h + gate * proj, gf_ref[...])


def _final(h1, y, p2, g3, wg, wp, gf, tm):
    t = h1.shape[0]
    tok = lambda w: pl.BlockSpec((tm, w), lambda i: (i, 0))
    full = lambda a: pl.BlockSpec(a.shape, lambda i: (0, 0))
    return pl.pallas_call(
        _final_kernel,
        grid=(t // tm,),
        in_specs=[tok(D_MODEL), tok(D_MODEL), tok(PLE_DIM), full(g3), full(wg), full(wp), full(gf)],
        out_specs=tok(D_MODEL),
        out_shape=jax.ShapeDtypeStruct((t, D_MODEL), F32),
        compiler_params=_params("parallel"),
        name="ple_final",
    )(h1, y, p2, g3, wg, wp, gf)


_TM_PROJ = 512
_TM_SELECT = 128
_TM_FFN = 1024


def _double_heads(w):
    d = w.shape[0]
    w = w.reshape(d, N_KV_HEADS, 1, HEAD_DIM)
    return jnp.broadcast_to(w, (d, N_KV_HEADS, 2, HEAD_DIM)).reshape(d, 2 * KV_W)


def kernel(x, p, norm_mix_g, w_in, attn_sink, conv_w, conv_b, lru_wa, lru_ba, lru_wx, lru_bx, lru_lambda,
           w_up_attn, w_up_lru, w_o, norm_ffn_g, peer_wq, peer_k1, peer_k2, peer_u, peer_v, norm_ple_g,
           ple_w_gate, ple_w_proj, final_g):
    bsz, seq, d = x.shape
    t = bsz * seq
    assert p.shape[0] == 1, "single-layer block: the final norm follows layer 0"
    h = x.reshape(t, d)
    for l in range(1):
        row = lambda a: a.reshape(1, -1)
        wi = w_in[l]
        w_comb = jnp.concatenate(
            [wi[:, :Q_W], _double_heads(wi[:, Q_W:Q_W + KV_W]), _double_heads(wi[:, Q_W + KV_W:Q_W + 2 * KV_W]),
             wi[:, Q_W + 2 * KV_W:]], axis=1).astype(BF16)
        q, kd, vd, lx, lg, ga, gl, u_bf16 = _in_proj(h, row(norm_mix_g[l]), w_comb, peer_u[l], _TM_PROJ)
        attn = _attention(attn_sink[l], q.reshape(bsz, seq, -1), kd.reshape(bsz, seq, -1),
                          vd.reshape(bsz, seq, -1)).reshape(t, Q_W)
        rec = _rg_lru(lx.reshape(bsz, seq, d), lg.reshape(bsz, seq, d), conv_w[l], row(conv_b[l]),
                      lru_wa[l].astype(BF16), row(lru_ba[l]), lru_wx[l].astype(BF16), row(lru_bx[l]),
                      row(lru_lambda[l])).reshape(t, d)
        h1, xn2, qp, v_bf16 = _merge(attn, rec, ga, gl, h, w_up_attn[l].astype(BF16), w_up_lru[l].astype(BF16),
                                     w_o[l].astype(BF16), row(norm_ffn_g[l]), peer_wq[l].astype(BF16),
                                     peer_v[l], _TM_PROJ)
        w = _peer_select(qp, peer_k1[l].astype(BF16), peer_k2[l].astype(BF16), _TM_SELECT)
        y = _peer_ffn(xn2, u_bf16, v_bf16, w, _TM_FFN)
        h = _final(h1, y, p[l].reshape(t, PLE_DIM), row(norm_ple_g[l]), ple_w_gate[l].astype(BF16),
                   ple_w_proj[l].astype(BF16), row(final_g), _TM_PROJ)
    return h.reshape(bsz, seq, d)
```

```python
import jax
import jax.numpy as jnp
from jax import lax
from jax.experimental import pallas as pl
from jax.experimental.pallas import tpu as pltpu

D_MODEL = 1024
PLE_DIM = 256
EPS = 1e-6
NEG_INF = -1e30
N_Q_HEADS = 8
N_KV_HEADS = 2
HEAD_DIM = 64
WINDOW = 128
BLOCK = 128
LRU_BLOCKS = 8
LRU_BLOCK_DIM = 128
CONV_WIDTH = 4
LRU_C = 8.0
PEER_HEADS = 8
PEER_KEYS = 128
PEER_HALF = 128
PEER_TOPK = 16
Q_W = N_Q_HEADS * HEAD_DIM
KV_W = N_KV_HEADS * HEAD_DIM

V7X_LANES = 128
V7X_SUBLANES = 8
V7X_VMEM_LIMIT_BYTES = 56 * 1024 * 1024

F32 = jnp.float32
BF16 = jnp.bfloat16
I32 = jnp.int32

_NT = (((1,), (1,)), ((), ()))


def _gelu(x):
    return 0.5 * x * (1.0 + jnp.tanh(0.7978845608028654 * (x + 0.044715 * (x * x * x))))


def _sigmoid(x):
    return 0.5 * (jnp.tanh(0.5 * x) + 1.0)


def _rms_norm(xf, g):
    ms = jnp.mean(xf * xf, axis=-1, keepdims=True)
    return xf * lax.rsqrt(ms + EPS) * g


def _params(*sem):
    return pltpu.CompilerParams(dimension_semantics=sem, vmem_limit_bytes=V7X_VMEM_LIMIT_BYTES)


_IN_COLS = (Q_W, 2 * KV_W, 2 * KV_W, D_MODEL, D_MODEL, D_MODEL, D_MODEL)


def _in_proj_kernel(x_ref, g_ref, w_ref, table_ref, *out_refs):
    *proj_refs, table_bf16_ref = out_refs
    xn = _rms_norm(x_ref[...], g_ref[...]).astype(BF16)
    c0 = 0
    for ref, width in zip(proj_refs, _IN_COLS):
        ref[...] = jnp.dot(xn, w_ref[:, c0:c0 + width], preferred_element_type=F32).astype(ref.dtype)
        c0 += width
    table_bf16_ref[...] = table_ref[...].astype(BF16)


def _table_specs(table, steps):
    rows = table.shape[0] // steps
    spec = pl.BlockSpec((rows, table.shape[1]), lambda i: (i, 0))
    return spec, jax.ShapeDtypeStruct(table.shape, BF16)


def _in_proj(x2, g, w, table, tm):
    t = x2.shape[0]
    n_in = sum(_IN_COLS)
    table_spec, table_shape = _table_specs(table, t // tm)
    return pl.pallas_call(
        _in_proj_kernel,
        grid=(t // tm,),
        in_specs=[
            pl.BlockSpec((tm, D_MODEL), lambda i: (i, 0)),
            pl.BlockSpec((1, D_MODEL), lambda i: (0, 0)),
            pl.BlockSpec((D_MODEL, n_in), lambda i: (0, 0)),
            table_spec,
        ],
        out_specs=[pl.BlockSpec((tm, c), lambda i: (i, 0)) for c in _IN_COLS] + [table_spec],
        out_shape=[jax.ShapeDtypeStruct((t, c), BF16) for c in _IN_COLS] + [table_shape],
        compiler_params=_params("parallel"),
        name="in_proj",
    )(x2, g, w, table)


_ATTN_BLOCKS = 8


def _attn_kernel(sink_ref, q_ref, kc_ref, kp_ref, vc_ref, vp_ref, o_ref):
    n = pl.program_id(1)
    qi = lax.broadcasted_iota(I32, (BLOCK, 2 * BLOCK), 0)
    kj = lax.broadcasted_iota(I32, (BLOCK, 2 * BLOCK), 1)
    dist_i = qi + BLOCK - kj
    dist = dist_i.astype(F32)
    in_window = (dist_i >= 0) & (dist_i < WINDOW)
    lane = lax.broadcasted_iota(I32, (BLOCK, V7X_LANES), 1)
    upper = lane >= HEAD_DIM
    scale = HEAD_DIM ** -0.5
    for j in range(_ATTN_BLOCKS):
        rows = slice(j * BLOCK, (j + 1) * BLOCK)
        before = slice((j - 1) * BLOCK, j * BLOCK)
        k_prev = kc_ref[before, :] if j else kp_ref[...]
        v_prev = vc_ref[before, :] if j else vp_ref[...]
        kk = jnp.concatenate([k_prev, kc_ref[rows, :]], axis=0)
        vv = jnp.concatenate([v_prev, vc_ref[rows, :]], axis=0)
        valid = in_window & (((n * _ATTN_BLOCKS + j) * BLOCK + kj - BLOCK) >= 0)
        for pair in range(N_Q_HEADS // 2):
            grp = (2 * pair) // (N_Q_HEADS // N_KV_HEADS)
            qp = q_ref[rows, pair * V7X_LANES:(pair + 1) * V7X_LANES]
            kk_g = kk[:, grp * V7X_LANES:(grp + 1) * V7X_LANES]
            vv_g = vv[:, grp * V7X_LANES:(grp + 1) * V7X_LANES]
            outs = []
            for sub in range(2):
                head = 2 * pair + sub
                keep = upper if sub else jnp.logical_not(upper)
                qm = jnp.where(keep, qp, jnp.zeros_like(qp))
                s = lax.dot_general(qm, kk_g, _NT, preferred_element_type=F32)
                slope = 2.0 ** (-8.0 * (head + 1) / N_Q_HEADS)
                s = jnp.where(valid, s * scale - slope * dist, NEG_INF)
                sk = sink_ref[head]
                m = jnp.maximum(jnp.max(s, axis=-1, keepdims=True), sk)
                e = jnp.exp(s - m)
                den = jnp.sum(e, axis=-1, keepdims=True) + jnp.exp(sk - m)
                p = (e * (1.0 / den)).astype(BF16)
                outs.append(jnp.dot(p, vv_g, preferred_element_type=F32))
            o_pair = jnp.where(upper, outs[1], outs[0])
            o_ref[rows, pair * V7X_LANES:(pair + 1) * V7X_LANES] = o_pair.astype(o_ref.dtype)


def _attention(sink, q3, k3, v3):
    b, s, _ = q3.shape
    rows = _ATTN_BLOCKS * BLOCK
    cur = lambda bi, n: (bi, n, 0)
    prev = lambda bi, n: (bi, jnp.maximum(n * _ATTN_BLOCKS - 1, 0), 0)
    return pl.pallas_call(
        _attn_kernel,
        grid=(b, s // rows),
        in_specs=[
            pl.BlockSpec(memory_space=pltpu.SMEM),
            pl.BlockSpec((None, rows, Q_W), cur),
            pl.BlockSpec((None, rows, 2 * KV_W), cur),
            pl.BlockSpec((None, BLOCK, 2 * KV_W), prev),
            pl.BlockSpec((None, rows, 2 * KV_W), cur),
            pl.BlockSpec((None, BLOCK, 2 * KV_W), prev),
        ],
        out_specs=pl.BlockSpec((None, rows, Q_W), cur),
        out_shape=jax.ShapeDtypeStruct((b, s, Q_W), BF16),
        compiler_params=_params("parallel", "parallel"),
        name="swa_attention",
    )(sink, q3, k3, k3, v3, v3)


def _lru_kernel(x_ref, gate_ref, cw_ref, cb_ref, wa_ref, ba_ref, wx_ref, bx_ref, lam_ref, o_ref):
    s_len = x_ref.shape[0]
    x = x_ref[...].astype(F32)
    row = lax.broadcasted_iota(I32, x.shape, 0)

    def shift(v, d, fill):
        return jnp.where(row >= d, pltpu.roll(v, d, axis=0), fill)

    cw = cw_ref[...]
    y = cb_ref[...] + shift(x, 3, 0.0) * cw[0:1]
    y = y + shift(x, 2, 0.0) * cw[1:2]
    y = y + shift(x, 1, 0.0) * cw[2:3]
    y = y + x * cw[3:4]
    yb = y.astype(BF16)
    tanh_r = jnp.tanh(jnp.dot(yb, wa_ref[...], preferred_element_type=F32) + ba_ref[...])
    tanh_i = jnp.tanh(jnp.dot(yb, wx_ref[...], preferred_element_type=F32) + bx_ref[...])
    z = -lam_ref[...]
    softplus = jnp.maximum(z, 0.0) + jnp.log1p(jnp.exp(-jnp.abs(z)))
    c = (-0.5 * LRU_C) * softplus
    log_a = c * tanh_r + c
    a = jnp.exp(log_a)
    z = 1.0 - a * a
    root = jnp.where(z > 0.0, z * lax.rsqrt(z), 0.0)
    half_y = 0.5 * y
    b = root * (half_y * tanh_i + half_y)
    grp = V7X_SUBLANES
    a = a.reshape(s_len // grp, grp, -1)
    b = b.reshape(a.shape)
    sub = lax.broadcasted_iota(I32, a.shape, 1)
    d = 1
    while d < grp:
        inside = sub >= d
        b = a * jnp.where(inside, pltpu.roll(b, d, axis=1), 0.0) + b
        a = a * jnp.where(inside, pltpu.roll(a, d, axis=1), 1.0)
        d *= 2
    a = a.reshape(x.shape)
    b = b.reshape(x.shape)
    gate = _gelu(gate_ref[...].astype(F32))
    carry = jnp.zeros((1, x.shape[1]), F32)
    step = 2 * grp
    for t0 in range(0, s_len, step):
        hs = []
        for t in range(t0, t0 + step, grp):
            h = a[t:t + grp] * carry + b[t:t + grp]
            carry = h[grp - 1:grp]
            hs.append(h)
        o_ref[t0:t0 + step, :] = (jnp.concatenate(hs, axis=0) * gate[t0:t0 + step]).astype(o_ref.dtype)


def _rg_lru(lx3, lg3, conv_w, conv_b, wa, ba, wx, bx, lam):
    b, s, c = lx3.shape
    bd = LRU_BLOCK_DIM
    act = lambda bi, ci: (bi, 0, ci)
    vec = lambda bi, ci: (0, ci)
    mat = lambda bi, ci: (ci, 0, 0)
    return pl.pallas_call(
        _lru_kernel,
        grid=(b, c // bd),
        in_specs=[
            pl.BlockSpec((None, s, bd), act),
            pl.BlockSpec((None, s, bd), act),
            pl.BlockSpec((CONV_WIDTH, bd), vec),
            pl.BlockSpec((1, bd), vec),
            pl.BlockSpec((None, bd, bd), mat),
            pl.BlockSpec((1, bd), vec),
            pl.BlockSpec((None, bd, bd), mat),
            pl.BlockSpec((1, bd), vec),
            pl.BlockSpec((1, bd), vec),
        ],
        out_specs=pl.BlockSpec((None, s, bd), act),
        out_shape=jax.ShapeDtypeStruct((b, s, c), BF16),
        compiler_params=_params("parallel", "parallel"),
        name="rg_lru",
    )(lx3, lg3, conv_w, conv_b, wa, ba, wx, bx, lam)


def _merge_kernel(attn_ref, rec_ref, ga_ref, gl_ref, x_ref, wua_ref, wul_ref, wo_ref, g2_ref, wq_ref,
                  table_ref, h_ref, xn_ref, qp_ref, table_bf16_ref):
    table_bf16_ref[...] = table_ref[...].astype(BF16)
    up_a = jnp.dot(attn_ref[...], wua_ref[...], preferred_element_type=F32)
    up_l = jnp.dot(rec_ref[...], wul_ref[...], preferred_element_type=F32)
    merged = _sigmoid(ga_ref[...].astype(F32)) * up_a + _sigmoid(gl_ref[...].astype(F32)) * up_l
    h = x_ref[...] + jnp.dot(merged.astype(BF16), wo_ref[...], preferred_element_type=F32)
    h_ref[...] = h
    xn = _rms_norm(h, g2_ref[...]).astype(BF16)
    xn_ref[...] = xn
    qp_ref[...] = jnp.dot(xn, wq_ref[...], preferred_element_type=F32).astype(qp_ref.dtype)


def _merge(attn, rec, ga, gl, x2, wua, wul, wo, g2, wq, table, tm):
    t = x2.shape[0]
    nq = wq.shape[1]
    tok = lambda w: pl.BlockSpec((tm, w), lambda i: (i, 0))
    full = lambda a: pl.BlockSpec(a.shape, lambda i: (0, 0))
    table_spec, table_shape = _table_specs(table, t // tm)
    return pl.pallas_call(
        _merge_kernel,
        grid=(t // tm,),
        in_specs=[tok(Q_W), tok(D_MODEL), tok(D_MODEL), tok(D_MODEL), tok(D_MODEL),
                  full(wua), full(wul), full(wo), full(g2), full(wq), table_spec],
        out_specs=[tok(D_MODEL), tok(D_MODEL), tok(nq), table_spec],
        out_shape=[jax.ShapeDtypeStruct((t, D_MODEL), F32),
                   jax.ShapeDtypeStruct((t, D_MODEL), BF16),
                   jax.ShapeDtypeStruct((t, nq), BF16),
                   table_shape],
        compiler_params=_params("parallel"),
        name="merge_proj",
    )(attn, rec, ga, gl, x2, wua, wul, wo, g2, wq, table)


def _oddeven_merge(lo, hi, r):
    step = r * 2
    if step < hi - lo:
        yield from _oddeven_merge(lo, hi, step)
        yield from _oddeven_merge(lo + r, hi, step)
        yield from [(i, i + r) for i in range(lo + r, hi - r, step)]
    else:
        yield (lo, lo + r)


def _oddeven_merge_sort(lo, hi):
    if hi - lo >= 1:
        mid = lo + (hi - lo) // 2
        yield from _oddeven_merge_sort(lo, mid)
        yield from _oddeven_merge_sort(mid + 1, hi)
        yield from _oddeven_merge(lo, hi, 1)


_SORT16 = tuple(_oddeven_merge_sort(0, PEER_TOPK - 1))
_GROUPS_AHEAD = 2


def _no_tick():
    return None


def _zero_scalar(items):
    bits = lax.bitcast_convert_type(items[0][1], jnp.uint32)
    zeros = lax.bitcast_convert_type((bits >> 16) >> 16, F32)
    return jnp.max(zeros).astype(I32)


class _Network:
    def __init__(self, tick=_no_tick):
        self.tick = tick

    def precedes(self, a, b):
        va, pa = a
        vb, pb = b
        if isinstance(pa, int) and isinstance(pb, int):
            return (va >= vb) if pa < pb else (va > vb)
        return (va > vb) | ((va == vb) & (pa < pb))

    def first(self, a, b):
        if b is None:
            return a
        if a is None:
            return b
        return jnp.maximum(a[0], b[0]), jnp.where(self.precedes(a, b), a[1], b[1])

    def exchange(self, a, b):
        if b is None:
            return a, None
        if a is None:
            return b, None
        c = self.precedes(a, b)
        first_payload = jnp.where(c, a[1], b[1])
        zero = self.tick()
        if zero is not None:
            first_payload = first_payload + zero
        return ((jnp.maximum(a[0], b[0]), first_payload),
                (jnp.minimum(a[0], b[0]), jnp.where(c, b[1], a[1])))

    def sort16(self, items):
        items = list(items)
        for i, j in _SORT16:
            items[i], items[j] = self.exchange(items[i], items[j])
        return items

    def merge16(self, xs, ys):
        n = PEER_TOPK
        xs = list(xs) + [None] * (n - len(xs))
        ys = list(ys) + [None] * (n - len(ys))
        h = [self.first(xs[i], ys[n - 1 - i]) for i in range(n)]
        d = n // 2
        while d:
            for i in range(n):
                if not i & d:
                    h[i], h[i + d] = self.exchange(h[i], h[i + d])
            d //= 2
        return [x for x in h if x is not None]

    def top16_of_keys(self, s_ref, after=None):
        runs = []
        earlier = [after] * _GROUPS_AHEAD
        for g in range(PEER_KEYS // PEER_TOPK):
            keys = range(g * PEER_TOPK, (g + 1) * PEER_TOPK)
            base = 0 if earlier[g] is None else _zero_scalar(earlier[g])
            tile = lambda k: jnp.concatenate(
                [s_ref[c, pl.ds(k + base, PEER_HEADS, stride=PEER_KEYS), :] for c in range(s_ref.shape[0])],
                axis=1)
            earlier.append(self.sort16([(tile(k), k) for k in keys]))
            runs.append((1, earlier[-1]))
            while len(runs) > 1 and runs[-1][0] == runs[-2][0]:
                (n, ys), (_, xs) = runs.pop(), runs.pop()
                runs.append((2 * n, self.merge16(xs, ys)))
        runs = [run for _, run in runs]
        while len(runs) > 1:
            runs = [self.merge16(runs[i], runs[i + 1]) for i in range(0, len(runs), 2)]
        return runs[0]


_EXPERT_BITS = 14
_EXCHANGES_PER_TOKEN = 12
_TOKENS_IN_FLIGHT = 12


def _scores(qp_ref, k1_ref, k2_ref, s1_scr, s2_scr):
    for h in range(PEER_HEADS):
        off = h * 2 * PEER_HALF
        rows = pl.ds(h * PEER_KEYS, PEER_KEYS)
        s1 = lax.dot_general(k1_ref[...], qp_ref[:, off:off + PEER_HALF], _NT, preferred_element_type=F32)
        s2 = lax.dot_general(k2_ref[...], qp_ref[:, off + PEER_HALF:off + 2 * PEER_HALF], _NT,
                             preferred_element_type=F32)
        for c in range(s1_scr.shape[0]):
            lanes = slice(c * V7X_LANES, (c + 1) * V7X_LANES)
            s1_scr[c, rows, :] = s1[:, lanes]
            s2_scr[c, rows, :] = s2[:, lanes]


def _retrieve(net, s1_scr, s2_scr):
    top1 = net.top16_of_keys(s1_scr)
    top2 = net.top16_of_keys(s2_scr, after=top1)

    def pair(j, l):
        expert = top1[j][1] * PEER_KEYS + top2[l][1]
        return top1[j][0] + top2[l][0], expert + ((j * PEER_TOPK + l) << _EXPERT_BITS)

    half = PEER_TOPK // 2
    rows = [[pair(j, l) for l in range(PEER_TOPK // (j + 1))] for j in range(half)]
    col = [pair(j, 0) for j in range(half, PEER_TOPK)]
    merge = net.merge16
    small = merge(merge(rows[2], rows[3]), merge(merge(rows[4], rows[5]), merge(rows[6], rows[7])))
    top = merge(rows[0], merge(merge(rows[1], col), small))
    ex = [jnp.exp(v - top[0][0]) for v, _ in top]
    den = ex[0]
    for x in ex[1:]:
        den = den + x
    gates = jnp.concatenate([x / den for x in ex], axis=0)
    experts = jnp.concatenate([p & ((1 << _EXPERT_BITS) - 1) for _, p in top], axis=0)
    return gates.T, experts.T


def _gate_matrix(g, e, key):
    a = e >> (_EXPERT_BITS // 2)
    b = e & (PEER_KEYS - 1)
    sel_a = jnp.where(key == a, g, 0.0).astype(BF16)
    sel_b = jnp.where(key == b, 1.0, 0.0).astype(BF16)
    return lax.dot_general(sel_a, sel_b, _NT, preferred_element_type=F32)


def _select_kernel(qp_ref, k1_ref, k2_ref, w_ref, s1_scr, s2_scr, g_scr, e_scr):
    tm = qp_ref.shape[0]

    @pl.when(pl.program_id(0) == 0)
    def _():
        g_scr[...] = jnp.zeros_like(g_scr)
        e_scr[...] = jnp.zeros_like(e_scr)

    key = lax.broadcasted_iota(I32, (PEER_KEYS, g_scr.shape[1]), 0)
    done = []
    ticks = [0]

    def gate_matrix_of_next_token():
        t = len(done)
        w = _gate_matrix(g_scr[t:t + 1, :], e_scr[t:t + 1, :], key)
        w_ref[:, t * V7X_SUBLANES:(t + 1) * V7X_SUBLANES, :] = w.reshape(
            PEER_KEYS // V7X_SUBLANES, V7X_SUBLANES, PEER_KEYS)
        done.append(w)

    def tick():
        ticks[0] += 1
        if ticks[0] % _EXCHANGES_PER_TOKEN:
            return None
        if len(done) < tm:
            gate_matrix_of_next_token()
        waited = ticks[0] // _EXCHANGES_PER_TOKEN - 1 - _TOKENS_IN_FLIGHT
        if not 0 <= waited < len(done):
            return None
        bits = lax.bitcast_convert_type(done[waited][:V7X_SUBLANES], jnp.uint32)
        zero = lax.bitcast_convert_type((bits >> 16) >> 16, I32)
        return jnp.concatenate([zero] * (tm // zero.shape[1]), axis=1)

    _scores(qp_ref, k1_ref, k2_ref, s1_scr, s2_scr)
    gates, experts = _retrieve(_Network(tick), s1_scr, s2_scr)
    while len(done) < tm:
        gate_matrix_of_next_token()
    g_scr[...] = gates
    e_scr[...] = experts


def _peer_select(qp, k1, k2, tm):
    t = qp.shape[0]
    nt = t // tm
    nsel = PEER_HEADS * PEER_TOPK
    na = PEER_KEYS // V7X_SUBLANES
    keys = pl.BlockSpec((PEER_KEYS, PEER_HALF), lambda i: (0, 0))
    return pl.pallas_call(
        _select_kernel,
        grid=(nt + 1,),
        in_specs=[pl.BlockSpec((tm, qp.shape[1]), lambda i: (jnp.minimum(i, nt - 1), 0)), keys, keys],
        out_specs=pl.BlockSpec((na, tm * V7X_SUBLANES, PEER_KEYS), lambda i: (0, jnp.maximum(i - 1, 0), 0)),
        out_shape=jax.ShapeDtypeStruct((na, t * V7X_SUBLANES, PEER_KEYS), F32),
        scratch_shapes=[pltpu.VMEM((tm // V7X_LANES, PEER_HEADS * PEER_KEYS, V7X_LANES), F32),
                        pltpu.VMEM((tm // V7X_LANES, PEER_HEADS * PEER_KEYS, V7X_LANES), F32),
                        pltpu.VMEM((tm, nsel), F32), pltpu.VMEM((tm, nsel), I32)],
        compiler_params=_params("arbitrary"),
        name="peer_select",
    )(qp, k1, k2)


def _ffn_kernel(x_ref, u_ref, v_ref, w_ref, y_ref):
    tm = x_ref.shape[0]

    @pl.when(pl.program_id(1) == 0)
    def _():
        y_ref[...] = jnp.zeros_like(y_ref)

    g = lax.dot_general(x_ref[...], u_ref[...], _NT, preferred_element_type=F32)
    hs = []
    for al in range(V7X_SUBLANES):
        wa = w_ref[pl.ds(al, tm, stride=V7X_SUBLANES), :]
        hs.append((wa * _gelu(g[:, al * PEER_KEYS:(al + 1) * PEER_KEYS])).astype(BF16))
    h = jnp.concatenate(hs, axis=1)
    y_ref[...] += jnp.dot(h, v_ref[...], preferred_element_type=F32)


def _peer_ffn(xn, u, v, w, tm):
    t = xn.shape[0]
    ne = V7X_SUBLANES * PEER_KEYS
    return pl.pallas_call(
        _ffn_kernel,
        grid=(t // tm, u.shape[0] // ne),
        in_specs=[
            pl.BlockSpec((tm, D_MODEL), lambda i, n: (i, 0)),
            pl.BlockSpec((ne, D_MODEL), lambda i, n: (n, 0)),
            pl.BlockSpec((ne, D_MODEL), lambda i, n: (n, 0)),
            pl.BlockSpec((None, tm * V7X_SUBLANES, PEER_KEYS), lambda i, n: (n, i, 0)),
        ],
        out_specs=pl.BlockSpec((tm, D_MODEL), lambda i, n: (i, 0)),
        out_shape=jax.ShapeDtypeStruct((t, D_MODEL), F32),
        compiler_params=_params("parallel", "arbitrary"),
        name="peer_ffn",
    )(xn, u, v, w)


def _final_kernel(h_ref, y_ref, p_ref, g3_ref, wg_ref, wp_ref, gf_ref, o_ref):
    h = h_ref[...] + y_ref[...]
    xn = _rms_norm(h, g3_ref[...]).astype(BF16)
    gate = _sigmoid(jnp.dot(xn, wg_ref[...], preferred_element_type=F32))
    proj = jnp.dot(p_ref[...].astype(BF16), wp_ref[...], preferred_element_type=F32)
    o_ref[...] = _rms_norm(h + gate * proj, gf_ref[...])


def _final(h1, y, p2, g3, wg, wp, gf, tm):
    t = h1.shape[0]
    tok = lambda w: pl.BlockSpec((tm, w), lambda i: (i, 0))
    full = lambda a: pl.BlockSpec(a.shape, lambda i: (0, 0))
    return pl.pallas_call(
        _final_kernel,
        grid=(t // tm,),
        in_specs=[tok(D_MODEL), tok(D_MODEL), tok(PLE_DIM), full(g3), full(wg), full(wp), full(gf)],
        out_specs=tok(D_MODEL),
        out_shape=jax.ShapeDtypeStruct((t, D_MODEL), F32),
        compiler_params=_params("parallel"),
        name="ple_final",
    )(h1, y, p2, g3, wg, wp, gf)


_TM_PROJ = 512
_TM_SELECT = 128
_TM_FFN = 1024
_TM_FINAL = 1024


def _double_heads(w):
    d = w.shape[0]
    w = w.reshape(d, N_KV_HEADS, 1, HEAD_DIM)
    return jnp.broadcast_to(w, (d, N_KV_HEADS, 2, HEAD_DIM)).reshape(d, 2 * KV_W)


def kernel(x, p, norm_mix_g, w_in, attn_sink, conv_w, conv_b, lru_wa, lru_ba, lru_wx, lru_bx, lru_lambda,
           w_up_attn, w_up_lru, w_o, norm_ffn_g, peer_wq, peer_k1, peer_k2, peer_u, peer_v, norm_ple_g,
           ple_w_gate, ple_w_proj, final_g):
    bsz, seq, d = x.shape
    t = bsz * seq
    assert p.shape[0] == 1, "single-layer block: the final norm follows layer 0"
    h = x.reshape(t, d)
    for l in range(1):
        row = lambda a: a.reshape(1, -1)
        wi = w_in[l]
        w_comb = jnp.concatenate(
            [wi[:, :Q_W], _double_heads(wi[:, Q_W:Q_W + KV_W]), _double_heads(wi[:, Q_W + KV_W:Q_W + 2 * KV_W]),
             wi[:, Q_W + 2 * KV_W:]], axis=1).astype(BF16)
        q, kd, vd, lx, lg, ga, gl, u_bf16 = _in_proj(h, row(norm_mix_g[l]), w_comb, peer_u[l], _TM_PROJ)
        attn = _attention(attn_sink[l], q.reshape(bsz, seq, -1), kd.reshape(bsz, seq, -1),
                          vd.reshape(bsz, seq, -1)).reshape(t, Q_W)
        rec = _rg_lru(lx.reshape(bsz, seq, d), lg.reshape(bsz, seq, d), conv_w[l], row(conv_b[l]),
                      (0.5 * lru_wa[l]).astype(BF16), row(0.5 * lru_ba[l]),
                      (0.5 * lru_wx[l]).astype(BF16), row(0.5 * lru_bx[l]),
                      row(lru_lambda[l])).reshape(t, d)
        h1, xn2, qp, v_bf16 = _merge(attn, rec, ga, gl, h, w_up_attn[l].astype(BF16), w_up_lru[l].astype(BF16),
                                     w_o[l].astype(BF16), row(norm_ffn_g[l]), peer_wq[l].astype(BF16),
                                     peer_v[l], _TM_PROJ)
        w = _peer_select(qp, peer_k1[l].astype(BF16), peer_k2[l].astype(BF16), _TM_SELECT)
        y = _peer_ffn(xn2, u_bf16, v_bf16, w, _TM_FFN)
        h = _final(h1, y, p[l].reshape(t, PLE_DIM), row(norm_ple_g[l]), ple_w_gate[l].astype(BF16),
                   ple_w_proj[l].astype(BF16), row(final_g), _TM_FINAL)
    return h.reshape(bsz, seq, d)
```

```python
import jax
import jax.numpy as jnp
from jax import lax
from jax.experimental import pallas as pl
from jax.experimental.pallas import tpu as pltpu

D_MODEL = 1024
PLE_DIM = 256
EPS = 1e-6
NEG_INF = -1e30
N_Q_HEADS = 8
N_KV_HEADS = 2
HEAD_DIM = 64
WINDOW = 128
BLOCK = 128
LRU_BLOCKS = 8
LRU_BLOCK_DIM = 128
CONV_WIDTH = 4
LRU_C = 8.0
PEER_HEADS = 8
PEER_KEYS = 128
PEER_HALF = 128
PEER_TOPK = 16
Q_W = N_Q_HEADS * HEAD_DIM
KV_W = N_KV_HEADS * HEAD_DIM

V7X_LANES = 128
V7X_SUBLANES = 8
V7X_VMEM_LIMIT_BYTES = 56 * 1024 * 1024

F32 = jnp.float32
BF16 = jnp.bfloat16
I32 = jnp.int32

_NT = (((1,), (1,)), ((), ()))


def _gelu(x):
    return 0.5 * x * (1.0 + jnp.tanh(0.7978845608028654 * (x + 0.044715 * (x * x * x))))


def _sigmoid(x):
    return 0.5 * (jnp.tanh(0.5 * x) + 1.0)


def _rms_norm(xf, g):
    ms = jnp.mean(xf * xf, axis=-1, keepdims=True)
    return xf * lax.rsqrt(ms + EPS) * g


def _params(*sem):
    return pltpu.CompilerParams(dimension_semantics=sem, vmem_limit_bytes=V7X_VMEM_LIMIT_BYTES)


_IN_COLS = (Q_W, 2 * KV_W, 2 * KV_W, D_MODEL, D_MODEL, D_MODEL, D_MODEL)


def _in_proj_kernel(x_ref, g_ref, w_ref, table_ref, *out_refs):
    *proj_refs, table_bf16_ref = out_refs
    xn = _rms_norm(x_ref[...], g_ref[...]).astype(BF16)
    c0 = 0
    for ref, width in zip(proj_refs, _IN_COLS):
        ref[...] = jnp.dot(xn, w_ref[:, c0:c0 + width], preferred_element_type=F32).astype(ref.dtype)
        c0 += width
    table_bf16_ref[...] = table_ref[...].astype(BF16)


def _table_specs(table, steps):
    rows = table.shape[0] // steps
    spec = pl.BlockSpec((rows, table.shape[1]), lambda i: (i, 0))
    return spec, jax.ShapeDtypeStruct(table.shape, BF16)


def _in_proj(x2, g, w, table, tm):
    t = x2.shape[0]
    n_in = sum(_IN_COLS)
    table_spec, table_shape = _table_specs(table, t // tm)
    return pl.pallas_call(
        _in_proj_kernel,
        grid=(t // tm,),
        in_specs=[
            pl.BlockSpec((tm, D_MODEL), lambda i: (i, 0)),
            pl.BlockSpec((1, D_MODEL), lambda i: (0, 0)),
            pl.BlockSpec((D_MODEL, n_in), lambda i: (0, 0)),
            table_spec,
        ],
        out_specs=[pl.BlockSpec((tm, c), lambda i: (i, 0)) for c in _IN_COLS] + [table_spec],
        out_shape=[jax.ShapeDtypeStruct((t, c), BF16) for c in _IN_COLS] + [table_shape],
        compiler_params=_params("parallel"),
        name="in_proj",
    )(x2, g, w, table)


_ATTN_BLOCKS = 16


def _attn_kernel(sink_ref, q_ref, kc_ref, kp_ref, vc_ref, vp_ref, o_ref):
    n = pl.program_id(1)
    qi = lax.broadcasted_iota(I32, (BLOCK, 2 * BLOCK), 0)
    kj = lax.broadcasted_iota(I32, (BLOCK, 2 * BLOCK), 1)
    dist_i = qi + BLOCK - kj
    dist = dist_i.astype(F32)
    in_window = (dist_i >= 0) & (dist_i < WINDOW)
    lane = lax.broadcasted_iota(I32, (BLOCK, V7X_LANES), 1)
    upper = lane >= HEAD_DIM
    scale = HEAD_DIM ** -0.5
    for j in range(_ATTN_BLOCKS):
        rows = slice(j * BLOCK, (j + 1) * BLOCK)
        before = slice((j - 1) * BLOCK, j * BLOCK)
        k_prev = kc_ref[before, :] if j else kp_ref[...]
        v_prev = vc_ref[before, :] if j else vp_ref[...]
        kk = jnp.concatenate([k_prev, kc_ref[rows, :]], axis=0)
        vv = jnp.concatenate([v_prev, vc_ref[rows, :]], axis=0)
        valid = in_window & (((n * _ATTN_BLOCKS + j) * BLOCK + kj - BLOCK) >= 0)
        for pair in range(N_Q_HEADS // 2):
            grp = (2 * pair) // (N_Q_HEADS // N_KV_HEADS)
            qp = q_ref[rows, pair * V7X_LANES:(pair + 1) * V7X_LANES]
            kk_g = kk[:, grp * V7X_LANES:(grp + 1) * V7X_LANES]
            vv_g = vv[:, grp * V7X_LANES:(grp + 1) * V7X_LANES]
            outs = []
            for sub in range(2):
                head = 2 * pair + sub
                keep = upper if sub else jnp.logical_not(upper)
                qm = jnp.where(keep, qp, jnp.zeros_like(qp))
                s = lax.dot_general(qm, kk_g, _NT, preferred_element_type=F32)
                slope = 2.0 ** (-8.0 * (head + 1) / N_Q_HEADS)
                s = jnp.where(valid, s * scale - slope * dist, NEG_INF)
                sk = sink_ref[head]
                m = jnp.maximum(jnp.max(s, axis=-1, keepdims=True), sk)
                e = jnp.exp(s - m)
                den = jnp.sum(e, axis=-1, keepdims=True) + jnp.exp(sk - m)
                p = (e * (1.0 / den)).astype(BF16)
                outs.append(jnp.dot(p, vv_g, preferred_element_type=F32))
            o_pair = jnp.where(upper, outs[1], outs[0])
            o_ref[rows, pair * V7X_LANES:(pair + 1) * V7X_LANES] = o_pair.astype(o_ref.dtype)


def _attention(sink, q3, k3, v3):
    b, s, _ = q3.shape
    rows = _ATTN_BLOCKS * BLOCK
    cur = lambda bi, n: (bi, n, 0)
    prev = lambda bi, n: (bi, jnp.maximum(n * _ATTN_BLOCKS - 1, 0), 0)
    return pl.pallas_call(
        _attn_kernel,
        grid=(b, s // rows),
        in_specs=[
            pl.BlockSpec(memory_space=pltpu.SMEM),
            pl.BlockSpec((None, rows, Q_W), cur),
            pl.BlockSpec((None, rows, 2 * KV_W), cur),
            pl.BlockSpec((None, BLOCK, 2 * KV_W), prev),
            pl.BlockSpec((None, rows, 2 * KV_W), cur),
            pl.BlockSpec((None, BLOCK, 2 * KV_W), prev),
        ],
        out_specs=pl.BlockSpec((None, rows, Q_W), cur),
        out_shape=jax.ShapeDtypeStruct((b, s, Q_W), BF16),
        compiler_params=_params("parallel", "parallel"),
        name="swa_attention",
    )(sink, q3, k3, k3, v3, v3)


def _lru_kernel(x_ref, gate_ref, cw_ref, cb_ref, wa_ref, ba_ref, wx_ref, bx_ref, lam_ref, o_ref):
    s_len = x_ref.shape[0]
    x = x_ref[...].astype(F32)
    row = lax.broadcasted_iota(I32, x.shape, 0)

    def shift(v, d, fill):
        return jnp.where(row >= d, pltpu.roll(v, d, axis=0), fill)

    cw = cw_ref[...]
    y = cb_ref[...] + shift(x, 3, 0.0) * cw[0:1]
    y = y + shift(x, 2, 0.0) * cw[1:2]
    y = y + shift(x, 1, 0.0) * cw[2:3]
    y = y + x * cw[3:4]
    yb = y.astype(BF16)
    tanh_r = jnp.tanh(jnp.dot(yb, wa_ref[...], preferred_element_type=F32) + ba_ref[...])
    tanh_i = jnp.tanh(jnp.dot(yb, wx_ref[...], preferred_element_type=F32) + bx_ref[...])
    z = -lam_ref[...]
    softplus = jnp.maximum(z, 0.0) + jnp.log1p(jnp.exp(-jnp.abs(z)))
    c = (-0.5 * LRU_C) * softplus
    log_a = c * tanh_r + c
    a = jnp.exp(log_a)
    z = 1.0 - a * a
    root = jnp.where(z > 0.0, z * lax.rsqrt(z), 0.0)
    half_y = 0.5 * y
    b = root * (half_y * tanh_i + half_y)
    grp = V7X_SUBLANES
    a = a.reshape(s_len // grp, grp, -1)
    b = b.reshape(a.shape)
    sub = lax.broadcasted_iota(I32, a.shape, 1)
    d = 1
    while d < grp:
        inside = sub >= d
        b = a * jnp.where(inside, pltpu.roll(b, d, axis=1), 0.0) + b
        a = a * jnp.where(inside, pltpu.roll(a, d, axis=1), 1.0)
        d *= 2
    a = a.reshape(x.shape)
    b = b.reshape(x.shape)
    gate = _gelu(gate_ref[...].astype(F32))
    carry = jnp.zeros((1, x.shape[1]), F32)
    step = 2 * grp
    for t0 in range(0, s_len, step):
        hs = []
        for t in range(t0, t0 + step, grp):
            h = a[t:t + grp] * carry + b[t:t + grp]
            carry = h[grp - 1:grp]
            hs.append(h)
        o_ref[t0:t0 + step, :] = (jnp.concatenate(hs, axis=0) * gate[t0:t0 + step]).astype(o_ref.dtype)


def _rg_lru(lx3, lg3, conv_w, conv_b, wa, ba, wx, bx, lam):
    b, s, c = lx3.shape
    bd = LRU_BLOCK_DIM
    act = lambda bi, ci: (bi, 0, ci)
    vec = lambda bi, ci: (0, ci)
    mat = lambda bi, ci: (ci, 0, 0)
    return pl.pallas_call(
        _lru_kernel,
        grid=(b, c // bd),
        in_specs=[
            pl.BlockSpec((None, s, bd), act),
            pl.BlockSpec((None, s, bd), act),
            pl.BlockSpec((CONV_WIDTH, bd), vec),
            pl.BlockSpec((1, bd), vec),
            pl.BlockSpec((None, bd, bd), mat),
            pl.BlockSpec((1, bd), vec),
            pl.BlockSpec((None, bd, bd), mat),
            pl.BlockSpec((1, bd), vec),
            pl.BlockSpec((1, bd), vec),
        ],
        out_specs=pl.BlockSpec((None, s, bd), act),
        out_shape=jax.ShapeDtypeStruct((b, s, c), BF16),
        compiler_params=_params("parallel", "parallel"),
        name="rg_lru",
    )(lx3, lg3, conv_w, conv_b, wa, ba, wx, bx, lam)


def _merge_kernel(attn_ref, rec_ref, ga_ref, gl_ref, x_ref, wua_ref, wul_ref, wo_ref, g2_ref, wq_ref,
                  table_ref, h_ref, xn_ref, qp_ref, table_bf16_ref):
    table_bf16_ref[...] = table_ref[...].astype(BF16)
    up_a = jnp.dot(attn_ref[...], wua_ref[...], preferred_element_type=F32)
    up_l = jnp.dot(rec_ref[...], wul_ref[...], preferred_element_type=F32)
    merged = _sigmoid(ga_ref[...].astype(F32)) * up_a + _sigmoid(gl_ref[...].astype(F32)) * up_l
    h = x_ref[...] + jnp.dot(merged.astype(BF16), wo_ref[...], preferred_element_type=F32)
    h_ref[...] = h
    xn = _rms_norm(h, g2_ref[...]).astype(BF16)
    xn_ref[...] = xn
    qp_ref[...] = jnp.dot(xn, wq_ref[...], preferred_element_type=F32).astype(qp_ref.dtype)


def _merge(attn, rec, ga, gl, x2, wua, wul, wo, g2, wq, table, tm):
    t = x2.shape[0]
    nq = wq.shape[1]
    tok = lambda w: pl.BlockSpec((tm, w), lambda i: (i, 0))
    full = lambda a: pl.BlockSpec(a.shape, lambda i: (0, 0))
    table_spec, table_shape = _table_specs(table, t // tm)
    return pl.pallas_call(
        _merge_kernel,
        grid=(t // tm,),
        in_specs=[tok(Q_W), tok(D_MODEL), tok(D_MODEL), tok(D_MODEL), tok(D_MODEL),
                  full(wua), full(wul), full(wo), full(g2), full(wq), table_spec],
        out_specs=[tok(D_MODEL), tok(D_MODEL), tok(nq), table_spec],
        out_shape=[jax.ShapeDtypeStruct((t, D_MODEL), F32),
                   jax.ShapeDtypeStruct((t, D_MODEL), BF16),
                   jax.ShapeDtypeStruct((t, nq), BF16),
                   table_shape],
        compiler_params=_params("parallel"),
        name="merge_proj",
    )(attn, rec, ga, gl, x2, wua, wul, wo, g2, wq, table)


def _oddeven_merge(lo, hi, r):
    step = r * 2
    if step < hi - lo:
        yield from _oddeven_merge(lo, hi, step)
        yield from _oddeven_merge(lo + r, hi, step)
        yield from [(i, i + r) for i in range(lo + r, hi - r, step)]
    else:
        yield (lo, lo + r)


def _oddeven_merge_sort(lo, hi):
    if hi - lo >= 1:
        mid = lo + (hi - lo) // 2
        yield from _oddeven_merge_sort(lo, mid)
        yield from _oddeven_merge_sort(mid + 1, hi)
        yield from _oddeven_merge(lo, hi, 1)


_SORT16 = tuple(_oddeven_merge_sort(0, PEER_TOPK - 1))
_GROUPS_AHEAD = 2


def _no_tick():
    return None


def _zero_scalar(items):
    bits = lax.bitcast_convert_type(items[0][1], jnp.uint32)
    zeros = lax.bitcast_convert_type((bits >> 16) >> 16, F32)
    return jnp.max(zeros).astype(I32)


class _Network:
    def __init__(self, tick=_no_tick):
        self.tick = tick

    def precedes(self, a, b):
        va, pa = a
        vb, pb = b
        if isinstance(pa, int) and isinstance(pb, int):
            return (va >= vb) if pa < pb else (va > vb)
        return (va > vb) | ((va == vb) & (pa < pb))

    def first(self, a, b):
        if b is None:
            return a
        if a is None:
            return b
        return jnp.maximum(a[0], b[0]), jnp.where(self.precedes(a, b), a[1], b[1])

    def exchange(self, a, b):
        if b is None:
            return a, None
        if a is None:
            return b, None
        c = self.precedes(a, b)
        first_payload = jnp.where(c, a[1], b[1])
        zero = self.tick()
        if zero is not None:
            first_payload = first_payload + zero
        return ((jnp.maximum(a[0], b[0]), first_payload),
                (jnp.minimum(a[0], b[0]), jnp.where(c, b[1], a[1])))

    def sort16(self, items):
        items = list(items)
        for i, j in _SORT16:
            items[i], items[j] = self.exchange(items[i], items[j])
        return items

    def merge16(self, xs, ys):
        n = PEER_TOPK
        xs = list(xs) + [None] * (n - len(xs))
        ys = list(ys) + [None] * (n - len(ys))
        h = [self.first(xs[i], ys[n - 1 - i]) for i in range(n)]
        d = n // 2
        while d:
            for i in range(n):
                if not i & d:
                    h[i], h[i + d] = self.exchange(h[i], h[i + d])
            d //= 2
        return [x for x in h if x is not None]

    def top16_of_keys(self, s_ref, after=(None,) * _GROUPS_AHEAD):
        runs = []
        earlier = list(after)
        for g in range(PEER_KEYS // PEER_TOPK):
            keys = range(g * PEER_TOPK, (g + 1) * PEER_TOPK)
            base = 0 if earlier[g] is None else _zero_scalar(earlier[g])
            tile = lambda k: jnp.concatenate(
                [s_ref[c, pl.ds(k + base, PEER_HEADS, stride=PEER_KEYS), :] for c in range(s_ref.shape[0])],
                axis=1)
            earlier.append(self.sort16([(tile(k), k) for k in keys]))
            runs.append((1, earlier[-1]))
            while len(runs) > 1 and runs[-1][0] == runs[-2][0]:
                (n, ys), (_, xs) = runs.pop(), runs.pop()
                runs.append((2 * n, self.merge16(xs, ys)))
        runs = [run for _, run in runs]
        while len(runs) > 1:
            runs = [self.merge16(runs[i], runs[i + 1]) for i in range(0, len(runs), 2)]
        return runs[0], earlier[-_GROUPS_AHEAD:]


_EXPERT_BITS = 14
_EXCHANGES_PER_TOKEN = 12
_TOKENS_IN_FLIGHT = 12


def _scores(qp_ref, k1_ref, k2_ref, s1_scr, s2_scr):
    for h in range(PEER_HEADS):
        off = h * 2 * PEER_HALF
        rows = pl.ds(h * PEER_KEYS, PEER_KEYS)
        s1 = lax.dot_general(k1_ref[...], qp_ref[:, off:off + PEER_HALF], _NT, preferred_element_type=F32)
        s2 = lax.dot_general(k2_ref[...], qp_ref[:, off + PEER_HALF:off + 2 * PEER_HALF], _NT,
                             preferred_element_type=F32)
        for c in range(s1_scr.shape[0]):
            lanes = slice(c * V7X_LANES, (c + 1) * V7X_LANES)
            s1_scr[c, rows, :] = s1[:, lanes]
            s2_scr[c, rows, :] = s2[:, lanes]


def _retrieve(net, s1_scr, s2_scr):
    top1, last_groups = net.top16_of_keys(s1_scr)
    top2, _ = net.top16_of_keys(s2_scr, after=last_groups)

    def pair(j, l):
        expert = top1[j][1] * PEER_KEYS + top2[l][1]
        return top1[j][0] + top2[l][0], expert + ((j * PEER_TOPK + l) << _EXPERT_BITS)

    half = PEER_TOPK // 2
    rows = [[pair(j, l) for l in range(PEER_TOPK // (j + 1))] for j in range(half)]
    col = [pair(j, 0) for j in range(half, PEER_TOPK)]
    merge = net.merge16
    small = merge(merge(rows[2], rows[3]), merge(merge(rows[4], rows[5]), merge(rows[6], rows[7])))
    top = merge(rows[0], merge(merge(rows[1], col), small))
    ex = [jnp.exp(v - top[0][0]) for v, _ in top]
    den = ex[0]
    for x in ex[1:]:
        den = den + x
    gates = jnp.concatenate([x / den for x in ex], axis=0)
    experts = jnp.concatenate([p & ((1 << _EXPERT_BITS) - 1) for _, p in top], axis=0)
    return gates.T, experts.T


def _gate_matrix(g, e, key):
    a = e >> (_EXPERT_BITS // 2)
    b = e & (PEER_KEYS - 1)
    sel_a = jnp.where(key == a, g, 0.0).astype(BF16)
    sel_b = jnp.where(key == b, 1.0, 0.0).astype(BF16)
    return lax.dot_general(sel_a, sel_b, _NT, preferred_element_type=F32)


def _select_kernel(qp_ref, k1_ref, k2_ref, w_ref, s1_scr, s2_scr, g_scr, e_scr):
    tm = qp_ref.shape[0]

    @pl.when(pl.program_id(0) == 0)
    def _():
        g_scr[...] = jnp.zeros_like(g_scr)
        e_scr[...] = jnp.zeros_like(e_scr)

    key = lax.broadcasted_iota(I32, (PEER_KEYS, g_scr.shape[1]), 0)
    done = []
    ticks = [0]

    def gate_matrix_of_next_token():
        t = len(done)
        w = _gate_matrix(g_scr[t:t + 1, :], e_scr[t:t + 1, :], key)
        w_ref[:, t * V7X_SUBLANES:(t + 1) * V7X_SUBLANES, :] = w.reshape(
            PEER_KEYS // V7X_SUBLANES, V7X_SUBLANES, PEER_KEYS)
        done.append(w)

    def tick():
        ticks[0] += 1
        if ticks[0] % _EXCHANGES_PER_TOKEN:
            return None
        if len(done) < tm:
            gate_matrix_of_next_token()
        waited = ticks[0] // _EXCHANGES_PER_TOKEN - 1 - _TOKENS_IN_FLIGHT
        if not 0 <= waited < len(done):
            return None
        bits = lax.bitcast_convert_type(done[waited][:V7X_SUBLANES], jnp.uint32)
        zero = lax.bitcast_convert_type((bits >> 16) >> 16, I32)
        return jnp.concatenate([zero] * (tm // zero.shape[1]), axis=1)

    _scores(qp_ref, k1_ref, k2_ref, s1_scr, s2_scr)
    gates, experts = _retrieve(_Network(tick), s1_scr, s2_scr)
    while len(done) < tm:
        gate_matrix_of_next_token()
    g_scr[...] = gates
    e_scr[...] = experts


def _peer_select(qp, k1, k2, tm):
    t = qp.shape[0]
    nt = t // tm
    nsel = PEER_HEADS * PEER_TOPK
    na = PEER_KEYS // V7X_SUBLANES
    keys = pl.BlockSpec((PEER_KEYS, PEER_HALF), lambda i: (0, 0))
    return pl.pallas_call(
        _select_kernel,
        grid=(nt + 1,),
        in_specs=[pl.BlockSpec((tm, qp.shape[1]), lambda i: (jnp.minimum(i, nt - 1), 0)), keys, keys],
        out_specs=pl.BlockSpec((na, tm * V7X_SUBLANES, PEER_KEYS), lambda i: (0, jnp.maximum(i - 1, 0), 0)),
        out_shape=jax.ShapeDtypeStruct((na, t * V7X_SUBLANES, PEER_KEYS), F32),
        scratch_shapes=[pltpu.VMEM((tm // V7X_LANES, PEER_HEADS * PEER_KEYS, V7X_LANES), F32),
                        pltpu.VMEM((tm // V7X_LANES, PEER_HEADS * PEER_KEYS, V7X_LANES), F32),
                        pltpu.VMEM((tm, nsel), F32), pltpu.VMEM((tm, nsel), I32)],
        compiler_params=_params("arbitrary"),
        name="peer_select",
    )(qp, k1, k2)


def _ffn_kernel(x_ref, u_ref, v_ref, w_ref, y_ref):
    tm = x_ref.shape[0]

    @pl.when(pl.program_id(1) == 0)
    def _():
        y_ref[...] = jnp.zeros_like(y_ref)

    g = lax.dot_general(x_ref[...], u_ref[...], _NT, preferred_element_type=F32)
    hs = []
    for al in range(V7X_SUBLANES):
        wa = w_ref[pl.ds(al, tm, stride=V7X_SUBLANES), :]
        hs.append((wa * _gelu(g[:, al * PEER_KEYS:(al + 1) * PEER_KEYS])).astype(BF16))
    h = jnp.concatenate(hs, axis=1)
    y_ref[...] += jnp.dot(h, v_ref[...], preferred_element_type=F32)


def _peer_ffn(xn, u, v, w, tm):
    t = xn.shape[0]
    ne = V7X_SUBLANES * PEER_KEYS
    return pl.pallas_call(
        _ffn_kernel,
        grid=(t // tm, u.shape[0] // ne),
        in_specs=[
            pl.BlockSpec((tm, D_MODEL), lambda i, n: (i, 0)),
            pl.BlockSpec((ne, D_MODEL), lambda i, n: (n, 0)),
            pl.BlockSpec((ne, D_MODEL), lambda i, n: (n, 0)),
            pl.BlockSpec((None, tm * V7X_SUBLANES, PEER_KEYS), lambda i, n: (n, i, 0)),
        ],
        out_specs=pl.BlockSpec((tm, D_MODEL), lambda i, n: (i, 0)),
        out_shape=jax.ShapeDtypeStruct((t, D_MODEL), F32),
        compiler_params=_params("parallel", "arbitrary"),
        name="peer_ffn",
    )(xn, u, v, w)


def _final_kernel(h_ref, y_ref, p_ref, g3_ref, wg_ref, wp_ref, gf_ref, o_ref):
    h = h_ref[...] + y_ref[...]
    xn = _rms_norm(h, g3_ref[...]).astype(BF16)
    gate = _sigmoid(jnp.dot(xn, wg_ref[...], preferred_element_type=F32))
    proj = jnp.dot(p_ref[...].astype(BF16), wp_ref[...], preferred_element_type=F32)
    o_ref[...] = _rms_norm(h + gate * proj, gf_ref[...])


def _final(h1, y, p2, g3, wg, wp, gf, tm):
    t = h1.shape[0]
    tok = lambda w: pl.BlockSpec((tm, w), lambda i: (i, 0))
    full = lambda a: pl.BlockSpec(a.shape, lambda i: (0, 0))
    return pl.pallas_call(
        _final_kernel,
        grid=(t // tm,),
        in_specs=[tok(D_MODEL), tok(D_MODEL), tok(PLE_DIM), full(g3), full(wg), full(wp), full(gf)],
        out_specs=tok(D_MODEL),
        out_shape=jax.ShapeDtypeStruct((t, D_MODEL), F32),
        compiler_params=_params("parallel"),
        name="ple_final",
    )(h1, y, p2, g3, wg, wp, gf)


_TM_PROJ = 512
_TM_SELECT = 128
_TM_FFN = 1024
_TM_FINAL = 1024


def _double_heads(w):
    d = w.shape[0]
    w = w.reshape(d, N_KV_HEADS, 1, HEAD_DIM)
    return jnp.broadcast_to(w, (d, N_KV_HEADS, 2, HEAD_DIM)).reshape(d, 2 * KV_W)


def kernel(x, p, norm_mix_g, w_in, attn_sink, conv_w, conv_b, lru_wa, lru_ba, lru_wx, lru_bx, lru_lambda,
           w_up_attn, w_up_lru, w_o, norm_ffn_g, peer_wq, peer_k1, peer_k2, peer_u, peer_v, norm_ple_g,
           ple_w_gate, ple_w_proj, final_g):
    bsz, seq, d = x.shape
    t = bsz * seq
    assert p.shape[0] == 1, "single-layer block: the final norm follows layer 0"
    h = x.reshape(t, d)
    for l in range(1):
        row = lambda a: a.reshape(1, -1)
        wi = w_in[l]
        w_comb = jnp.concatenate(
            [wi[:, :Q_W], _double_heads(wi[:, Q_W:Q_W + KV_W]), _double_heads(wi[:, Q_W + KV_W:Q_W + 2 * KV_W]),
             wi[:, Q_W + 2 * KV_W:]], axis=1).astype(BF16)
        q, kd, vd, lx, lg, ga, gl, u_bf16 = _in_proj(h, row(norm_mix_g[l]), w_comb, peer_u[l], _TM_PROJ)
        attn = _attention(attn_sink[l], q.reshape(bsz, seq, -1), kd.reshape(bsz, seq, -1),
                          vd.reshape(bsz, seq, -1)).reshape(t, Q_W)
        rec = _rg_lru(lx.reshape(bsz, seq, d), lg.reshape(bsz, seq, d), conv_w[l], row(conv_b[l]),
                      (0.5 * lru_wa[l]).astype(BF16), row(0.5 * lru_ba[l]),
                      (0.5 * lru_wx[l]).astype(BF16), row(0.5 * lru_bx[l]),
                      row(lru_lambda[l])).reshape(t, d)
        h1, xn2, qp, v_bf16 = _merge(attn, rec, ga, gl, h, w_up_attn[l].astype(BF16), w_up_lru[l].astype(BF16),
                                     w_o[l].astype(BF16), row(norm_ffn_g[l]), peer_wq[l].astype(BF16),
                                     peer_v[l], _TM_PROJ)
        w = _peer_select(qp, peer_k1[l].astype(BF16), peer_k2[l].astype(BF16), _TM_SELECT)
        y = _peer_ffn(xn2, u_bf16, v_bf16, w, _TM_FFN)
        h = _final(h1, y, p[l].reshape(t, PLE_DIM), row(norm_ple_g[l]), ple_w_gate[l].astype(BF16),
                   ple_w_proj[l].astype(BF16), row(final_g), _TM_FINAL)
    return h.reshape(bsz, seq, d)
```

```python
import jax
import jax.numpy as jnp
from jax import lax
from jax.experimental import pallas as pl
from jax.experimental.pallas import tpu as pltpu

D_MODEL = 1024
PLE_DIM = 256
EPS = 1e-6
NEG_INF = -1e30
N_Q_HEADS = 8
N_KV_HEADS = 2
HEAD_DIM = 64
WINDOW = 128
BLOCK = 128
LRU_BLOCKS = 8
LRU_BLOCK_DIM = 128
CONV_WIDTH = 4
LRU_C = 8.0
PEER_HEADS = 8
PEER_KEYS = 128
PEER_HALF = 128
PEER_TOPK = 16
Q_W = N_Q_HEADS * HEAD_DIM
KV_W = N_KV_HEADS * HEAD_DIM

V7X_LANES = 128
V7X_SUBLANES = 8
V7X_VMEM_LIMIT_BYTES = 56 * 1024 * 1024

F32 = jnp.float32
BF16 = jnp.bfloat16
I32 = jnp.int32

_NT = (((1,), (1,)), ((), ()))


def _gelu(x):
    return 0.5 * x * (1.0 + jnp.tanh(0.7978845608028654 * (x + 0.044715 * (x * x * x))))


def _sigmoid(x):
    return 0.5 * (jnp.tanh(0.5 * x) + 1.0)


def _rms_norm(xf, g):
    ms = jnp.mean(xf * xf, axis=-1, keepdims=True)
    return xf * lax.rsqrt(ms + EPS) * g


def _params(*sem):
    return pltpu.CompilerParams(dimension_semantics=sem, vmem_limit_bytes=V7X_VMEM_LIMIT_BYTES)


_IN_COLS = (Q_W, 2 * KV_W, 2 * KV_W, D_MODEL, D_MODEL, D_MODEL, D_MODEL)


def _in_proj_kernel(x_ref, g_ref, w_ref, table_ref, *out_refs):
    *proj_refs, table_bf16_ref = out_refs
    xn = _rms_norm(x_ref[...], g_ref[...]).astype(BF16)
    c0 = 0
    for ref, width in zip(proj_refs, _IN_COLS):
        ref[...] = jnp.dot(xn, w_ref[:, c0:c0 + width], preferred_element_type=F32).astype(ref.dtype)
        c0 += width
    table_bf16_ref[...] = table_ref[...].astype(BF16)


def _table_specs(table, steps):
    rows = table.shape[0] // steps
    spec = pl.BlockSpec((rows, table.shape[1]), lambda i: (i, 0))
    return spec, jax.ShapeDtypeStruct(table.shape, BF16)


def _in_proj(x2, g, w, table, tm):
    t = x2.shape[0]
    n_in = sum(_IN_COLS)
    table_spec, table_shape = _table_specs(table, t // tm)
    return pl.pallas_call(
        _in_proj_kernel,
        grid=(t // tm,),
        in_specs=[
            pl.BlockSpec((tm, D_MODEL), lambda i: (i, 0)),
            pl.BlockSpec((1, D_MODEL), lambda i: (0, 0)),
            pl.BlockSpec((D_MODEL, n_in), lambda i: (0, 0)),
            table_spec,
        ],
        out_specs=[pl.BlockSpec((tm, c), lambda i: (i, 0)) for c in _IN_COLS] + [table_spec],
        out_shape=[jax.ShapeDtypeStruct((t, c), BF16) for c in _IN_COLS] + [table_shape],
        compiler_params=_params("parallel"),
        name="in_proj",
    )(x2, g, w, table)


_ATTN_BLOCKS = 16


def _attn_kernel(sink_ref, q_ref, kc_ref, kp_ref, vc_ref, vp_ref, o_ref):
    n = pl.program_id(1)
    qi = lax.broadcasted_iota(I32, (BLOCK, 2 * BLOCK), 0)
    kj = lax.broadcasted_iota(I32, (BLOCK, 2 * BLOCK), 1)
    dist_i = qi + BLOCK - kj
    dist = dist_i.astype(F32)
    in_window = (dist_i >= 0) & (dist_i < WINDOW)
    lane = lax.broadcasted_iota(I32, (BLOCK, V7X_LANES), 1)
    upper = lane >= HEAD_DIM
    scale = HEAD_DIM ** -0.5
    for j in range(_ATTN_BLOCKS):
        rows = slice(j * BLOCK, (j + 1) * BLOCK)
        before = slice((j - 1) * BLOCK, j * BLOCK)
        k_prev = kc_ref[before, :] if j else kp_ref[...]
        v_prev = vc_ref[before, :] if j else vp_ref[...]
        kk = jnp.concatenate([k_prev, kc_ref[rows, :]], axis=0)
        vv = jnp.concatenate([v_prev, vc_ref[rows, :]], axis=0)
        valid = in_window & (((n * _ATTN_BLOCKS + j) * BLOCK + kj - BLOCK) >= 0)
        for pair in range(N_Q_HEADS // 2):
            grp = (2 * pair) // (N_Q_HEADS // N_KV_HEADS)
            qp = q_ref[rows, pair * V7X_LANES:(pair + 1) * V7X_LANES]
            kk_g = kk[:, grp * V7X_LANES:(grp + 1) * V7X_LANES]
            vv_g = vv[:, grp * V7X_LANES:(grp + 1) * V7X_LANES]
            outs = []
            for sub in range(2):
                head = 2 * pair + sub
                keep = upper if sub else jnp.logical_not(upper)
                qm = jnp.where(keep, qp, jnp.zeros_like(qp))
                s = lax.dot_general(qm, kk_g, _NT, preferred_element_type=F32)
                slope = 2.0 ** (-8.0 * (head + 1) / N_Q_HEADS)
                s = jnp.where(valid, s * scale - slope * dist, NEG_INF)
                sk = sink_ref[head]
                m = jnp.maximum(jnp.max(s, axis=-1, keepdims=True), sk)
                e = jnp.exp(s - m)
                den = jnp.sum(e, axis=-1, keepdims=True) + jnp.exp(sk - m)
                p = (e * (1.0 / den)).astype(BF16)
                outs.append(jnp.dot(p, vv_g, preferred_element_type=F32))
            o_pair = jnp.where(upper, outs[1], outs[0])
            o_ref[rows, pair * V7X_LANES:(pair + 1) * V7X_LANES] = o_pair.astype(o_ref.dtype)


def _attention(sink, q3, k3, v3):
    b, s, _ = q3.shape
    rows = _ATTN_BLOCKS * BLOCK
    cur = lambda bi, n: (bi, n, 0)
    prev = lambda bi, n: (bi, jnp.maximum(n * _ATTN_BLOCKS - 1, 0), 0)
    return pl.pallas_call(
        _attn_kernel,
        grid=(b, s // rows),
        in_specs=[
            pl.BlockSpec(memory_space=pltpu.SMEM),
            pl.BlockSpec((None, rows, Q_W), cur),
            pl.BlockSpec((None, rows, 2 * KV_W), cur),
            pl.BlockSpec((None, BLOCK, 2 * KV_W), prev),
            pl.BlockSpec((None, rows, 2 * KV_W), cur),
            pl.BlockSpec((None, BLOCK, 2 * KV_W), prev),
        ],
        out_specs=pl.BlockSpec((None, rows, Q_W), cur),
        out_shape=jax.ShapeDtypeStruct((b, s, Q_W), BF16),
        compiler_params=_params("parallel", "parallel"),
        name="swa_attention",
    )(sink, q3, k3, k3, v3, v3)


def _lru_kernel(x_ref, gate_ref, cw_ref, cb_ref, wa_ref, ba_ref, wx_ref, bx_ref, lam_ref, o_ref):
    s_len = x_ref.shape[0]
    x = x_ref[...].astype(F32)
    row = lax.broadcasted_iota(I32, x.shape, 0)

    def shift(v, d, fill):
        return jnp.where(row >= d, pltpu.roll(v, d, axis=0), fill)

    cw = cw_ref[...]
    y = cb_ref[...] + shift(x, 3, 0.0) * cw[0:1]
    y = y + shift(x, 2, 0.0) * cw[1:2]
    y = y + shift(x, 1, 0.0) * cw[2:3]
    y = y + x * cw[3:4]
    yb = y.astype(BF16)
    tanh_r = jnp.tanh(jnp.dot(yb, wa_ref[...], preferred_element_type=F32) + ba_ref[...])
    tanh_i = jnp.tanh(jnp.dot(yb, wx_ref[...], preferred_element_type=F32) + bx_ref[...])
    z = -lam_ref[...]
    softplus = jnp.maximum(z, 0.0) + jnp.log1p(jnp.exp(-jnp.abs(z)))
    c = (-0.5 * LRU_C) * softplus
    log_a = c * tanh_r + c
    a = jnp.exp(log_a)
    z = 1.0 - a * a
    root = jnp.where(z > 0.0, z * lax.rsqrt(z), 0.0)
    half_y = 0.5 * y
    b = root * (half_y * tanh_i + half_y)
    grp = V7X_SUBLANES
    a = a.reshape(s_len // grp, grp, -1)
    b = b.reshape(a.shape)
    sub = lax.broadcasted_iota(I32, a.shape, 1)
    d = 1
    while d < grp:
        inside = sub >= d
        b = a * jnp.where(inside, pltpu.roll(b, d, axis=1), 0.0) + b
        a = a * jnp.where(inside, pltpu.roll(a, d, axis=1), 1.0)
        d *= 2
    a = a.reshape(x.shape)
    b = b.reshape(x.shape)
    gate = _gelu(gate_ref[...].astype(F32))
    carry = jnp.zeros((1, x.shape[1]), F32)
    step = 2 * grp
    for t0 in range(0, s_len, step):
        hs = []
        for t in range(t0, t0 + step, grp):
            h = a[t:t + grp] * carry + b[t:t + grp]
            carry = h[grp - 1:grp]
            hs.append(h)
        o_ref[t0:t0 + step, :] = (jnp.concatenate(hs, axis=0) * gate[t0:t0 + step]).astype(o_ref.dtype)


def _rg_lru(lx3, lg3, conv_w, conv_b, wa, ba, wx, bx, lam):
    b, s, c = lx3.shape
    bd = LRU_BLOCK_DIM
    act = lambda bi, ci: (bi, 0, ci)
    vec = lambda bi, ci: (0, ci)
    mat = lambda bi, ci: (ci, 0, 0)
    return pl.pallas_call(
        _lru_kernel,
        grid=(b, c // bd),
        in_specs=[
            pl.BlockSpec((None, s, bd), act),
            pl.BlockSpec((None, s, bd), act),
            pl.BlockSpec((CONV_WIDTH, bd), vec),
            pl.BlockSpec((1, bd), vec),
            pl.BlockSpec((None, bd, bd), mat),
            pl.BlockSpec((1, bd), vec),
            pl.BlockSpec((None, bd, bd), mat),
            pl.BlockSpec((1, bd), vec),
            pl.BlockSpec((1, bd), vec),
        ],
        out_specs=pl.BlockSpec((None, s, bd), act),
        out_shape=jax.ShapeDtypeStruct((b, s, c), BF16),
        compiler_params=_params("parallel", "parallel"),
        name="rg_lru",
    )(lx3, lg3, conv_w, conv_b, wa, ba, wx, bx, lam)


def _merge_kernel(attn_ref, rec_ref, ga_ref, gl_ref, x_ref, wua_ref, wul_ref, wo_ref, g2_ref, wq_ref,
                  table_ref, h_ref, xn_ref, qp_ref, table_bf16_ref):
    table_bf16_ref[...] = table_ref[...].astype(BF16)
    up_a = jnp.dot(attn_ref[...], wua_ref[...], preferred_element_type=F32)
    up_l = jnp.dot(rec_ref[...], wul_ref[...], preferred_element_type=F32)
    merged = _sigmoid(ga_ref[...].astype(F32)) * up_a + _sigmoid(gl_ref[...].astype(F32)) * up_l
    h = x_ref[...] + jnp.dot(merged.astype(BF16), wo_ref[...], preferred_element_type=F32)
    h_ref[...] = h
    xn = _rms_norm(h, g2_ref[...]).astype(BF16)
    xn_ref[...] = xn
    qp_ref[...] = jnp.dot(xn, wq_ref[...], preferred_element_type=F32).astype(qp_ref.dtype)


def _merge(attn, rec, ga, gl, x2, wua, wul, wo, g2, wq, table, tm):
    t = x2.shape[0]
    nq = wq.shape[1]
    tok = lambda w: pl.BlockSpec((tm, w), lambda i: (i, 0))
    full = lambda a: pl.BlockSpec(a.shape, lambda i: (0, 0))
    table_spec, table_shape = _table_specs(table, t // tm)
    return pl.pallas_call(
        _merge_kernel,
        grid=(t // tm,),
        in_specs=[tok(Q_W), tok(D_MODEL), tok(D_MODEL), tok(D_MODEL), tok(D_MODEL),
                  full(wua), full(wul), full(wo), full(g2), full(wq), table_spec],
        out_specs=[tok(D_MODEL), tok(D_MODEL), tok(nq), table_spec],
        out_shape=[jax.ShapeDtypeStruct((t, D_MODEL), F32),
                   jax.ShapeDtypeStruct((t, D_MODEL), BF16),
                   jax.ShapeDtypeStruct((t, nq), BF16),
                   table_shape],
        compiler_params=_params("parallel"),
        name="merge_proj",
    )(attn, rec, ga, gl, x2, wua, wul, wo, g2, wq, table)


def _oddeven_merge(lo, hi, r):
    step = r * 2
    if step < hi - lo:
        yield from _oddeven_merge(lo, hi, step)
        yield from _oddeven_merge(lo + r, hi, step)
        yield from [(i, i + r) for i in range(lo + r, hi - r, step)]
    else:
        yield (lo, lo + r)


def _oddeven_merge_sort(lo, hi):
    if hi - lo >= 1:
        mid = lo + (hi - lo) // 2
        yield from _oddeven_merge_sort(lo, mid)
        yield from _oddeven_merge_sort(mid + 1, hi)
        yield from _oddeven_merge(lo, hi, 1)


_SORT16 = tuple(_oddeven_merge_sort(0, PEER_TOPK - 1))
_SCORE_PITCH = PEER_KEYS + V7X_SUBLANES // 2
_GROUPS_AHEAD = 2


def _no_tick():
    return None


def _zero_scalar(items):
    bits = lax.bitcast_convert_type(items[0][1], jnp.uint32)
    zeros = lax.bitcast_convert_type((bits >> 16) >> 16, F32)
    return jnp.max(zeros).astype(I32)


class _Network:
    def __init__(self, tick=_no_tick):
        self.tick = tick

    def precedes(self, a, b):
        va, pa = a
        vb, pb = b
        if isinstance(pa, int) and isinstance(pb, int):
            return (va >= vb) if pa < pb else (va > vb)
        return (va > vb) | ((va == vb) & (pa < pb))

    def first(self, a, b):
        if b is None:
            return a
        if a is None:
            return b
        return jnp.maximum(a[0], b[0]), jnp.where(self.precedes(a, b), a[1], b[1])

    def exchange(self, a, b):
        if b is None:
            return a, None
        if a is None:
            return b, None
        c = self.precedes(a, b)
        first_payload = jnp.where(c, a[1], b[1])
        zero = self.tick()
        if zero is not None:
            first_payload = first_payload + zero
        return ((jnp.maximum(a[0], b[0]), first_payload),
                (jnp.minimum(a[0], b[0]), jnp.where(c, b[1], a[1])))

    def sort16(self, items):
        items = list(items)
        for i, j in _SORT16:
            items[i], items[j] = self.exchange(items[i], items[j])
        return items

    def merge16(self, xs, ys):
        n = PEER_TOPK
        xs = list(xs) + [None] * (n - len(xs))
        ys = list(ys) + [None] * (n - len(ys))
        h = [self.first(xs[i], ys[n - 1 - i]) for i in range(n)]
        d = n // 2
        while d:
            for i in range(n):
                if not i & d:
                    h[i], h[i + d] = self.exchange(h[i], h[i + d])
            d //= 2
        return [x for x in h if x is not None]

    def top16_of_keys(self, s_ref, after=(None,) * _GROUPS_AHEAD):
        runs = []
        earlier = list(after)
        for g in range(PEER_KEYS // PEER_TOPK):
            keys = range(g * PEER_TOPK, (g + 1) * PEER_TOPK)
            base = 0 if earlier[g] is None else _zero_scalar(earlier[g])
            tile = lambda k: jnp.concatenate(
                [s_ref[c, pl.ds(k + base, PEER_HEADS, stride=_SCORE_PITCH), :] for c in range(s_ref.shape[0])],
                axis=1)
            earlier.append(self.sort16([(tile(k), k) for k in keys]))
            runs.append((1, earlier[-1]))
            while len(runs) > 1 and runs[-1][0] == runs[-2][0]:
                (n, ys), (_, xs) = runs.pop(), runs.pop()
                runs.append((2 * n, self.merge16(xs, ys)))
        runs = [run for _, run in runs]
        while len(runs) > 1:
            runs = [self.merge16(runs[i], runs[i + 1]) for i in range(0, len(runs), 2)]
        return runs[0], earlier[-_GROUPS_AHEAD:]


_EXPERT_BITS = 14
_EXCHANGES_PER_TOKEN = 12
_TOKENS_IN_FLIGHT = 12


def _scores(qp_ref, k1_ref, k2_ref, s1_scr, s2_scr):
    for h in range(PEER_HEADS):
        off = h * 2 * PEER_HALF
        rows = pl.ds(h * _SCORE_PITCH, PEER_KEYS)
        s1 = lax.dot_general(k1_ref[...], qp_ref[:, off:off + PEER_HALF], _NT, preferred_element_type=F32)
        s2 = lax.dot_general(k2_ref[...], qp_ref[:, off + PEER_HALF:off + 2 * PEER_HALF], _NT,
                             preferred_element_type=F32)
        for c in range(s1_scr.shape[0]):
            lanes = slice(c * V7X_LANES, (c + 1) * V7X_LANES)
            s1_scr[c, rows, :] = s1[:, lanes]
            s2_scr[c, rows, :] = s2[:, lanes]


def _retrieve(net, s1_scr, s2_scr):
    top1, last_groups = net.top16_of_keys(s1_scr)
    top2, _ = net.top16_of_keys(s2_scr, after=last_groups)

    def pair(j, l):
        expert = top1[j][1] * PEER_KEYS + top2[l][1]
        return top1[j][0] + top2[l][0], expert + ((j * PEER_TOPK + l) << _EXPERT_BITS)

    half = PEER_TOPK // 2
    rows = [[pair(j, l) for l in range(PEER_TOPK // (j + 1))] for j in range(half)]
    col = [pair(j, 0) for j in range(half, PEER_TOPK)]
    merge = net.merge16
    small = merge(merge(rows[2], rows[3]), merge(merge(rows[4], rows[5]), merge(rows[6], rows[7])))
    top = merge(rows[0], merge(merge(rows[1], col), small))
    ex = [jnp.exp(v - top[0][0]) for v, _ in top]
    den = ex[0]
    for x in ex[1:]:
        den = den + x
    gates = jnp.concatenate([x / den for x in ex], axis=0)
    experts = jnp.concatenate([p & ((1 << _EXPERT_BITS) - 1) for _, p in top], axis=0)
    return gates.T, experts.T


def _gate_matrix(g, e, key):
    a = e >> (_EXPERT_BITS // 2)
    b = e & (PEER_KEYS - 1)
    sel_a = jnp.where(key == a, g, 0.0).astype(BF16)
    sel_b = jnp.where(key == b, 1.0, 0.0).astype(BF16)
    return lax.dot_general(sel_a, sel_b, _NT, preferred_element_type=F32)


def _select_kernel(qp_ref, k1_ref, k2_ref, w_ref, s1_scr, s2_scr, g_scr, e_scr):
    tm = qp_ref.shape[0]

    @pl.when(pl.program_id(0) == 0)
    def _():
        g_scr[...] = jnp.zeros_like(g_scr)
        e_scr[...] = jnp.zeros_like(e_scr)

    key = lax.broadcasted_iota(I32, (PEER_KEYS, g_scr.shape[1]), 0)
    done = []
    ticks = [0]

    def gate_matrix_of_next_token():
        t = len(done)
        w = _gate_matrix(g_scr[t:t + 1, :], e_scr[t:t + 1, :], key)
        w_ref[:, t * V7X_SUBLANES:(t + 1) * V7X_SUBLANES, :] = w.reshape(
            PEER_KEYS // V7X_SUBLANES, V7X_SUBLANES, PEER_KEYS)
        done.append(w)

    def tick():
        ticks[0] += 1
        if ticks[0] % _EXCHANGES_PER_TOKEN:
            return None
        if len(done) < tm:
            gate_matrix_of_next_token()
        waited = ticks[0] // _EXCHANGES_PER_TOKEN - 1 - _TOKENS_IN_FLIGHT
        if not 0 <= waited < len(done):
            return None
        bits = lax.bitcast_convert_type(done[waited][:V7X_SUBLANES], jnp.uint32)
        zero = lax.bitcast_convert_type((bits >> 16) >> 16, I32)
        return jnp.concatenate([zero] * (tm // zero.shape[1]), axis=1)

    _scores(qp_ref, k1_ref, k2_ref, s1_scr, s2_scr)
    gates, experts = _retrieve(_Network(tick), s1_scr, s2_scr)
    while len(done) < tm:
        gate_matrix_of_next_token()
    g_scr[...] = gates
    e_scr[...] = experts


def _peer_select(qp, k1, k2, tm):
    t = qp.shape[0]
    nt = t // tm
    nsel = PEER_HEADS * PEER_TOPK
    na = PEER_KEYS // V7X_SUBLANES
    keys = pl.BlockSpec((PEER_KEYS, PEER_HALF), lambda i: (0, 0))
    return pl.pallas_call(
        _select_kernel,
        grid=(nt + 1,),
        in_specs=[pl.BlockSpec((tm, qp.shape[1]), lambda i: (jnp.minimum(i, nt - 1), 0)), keys, keys],
        out_specs=pl.BlockSpec((na, tm * V7X_SUBLANES, PEER_KEYS), lambda i: (0, jnp.maximum(i - 1, 0), 0)),
        out_shape=jax.ShapeDtypeStruct((na, t * V7X_SUBLANES, PEER_KEYS), F32),
        scratch_shapes=[pltpu.VMEM((tm // V7X_LANES, PEER_HEADS * _SCORE_PITCH, V7X_LANES), F32),
                        pltpu.VMEM((tm // V7X_LANES, PEER_HEADS * _SCORE_PITCH, V7X_LANES), F32),
                        pltpu.VMEM((tm, nsel), F32), pltpu.VMEM((tm, nsel), I32)],
        compiler_params=_params("arbitrary"),
        name="peer_select",
    )(qp, k1, k2)


def _ffn_kernel(x_ref, u_ref, v_ref, w_ref, y_ref):
    tm = x_ref.shape[0]

    @pl.when(pl.program_id(1) == 0)
    def _():
        y_ref[...] = jnp.zeros_like(y_ref)

    g = lax.dot_general(x_ref[...], u_ref[...], _NT, preferred_element_type=F32)
    hs = []
    for al in range(V7X_SUBLANES):
        wa = w_ref[pl.ds(al, tm, stride=V7X_SUBLANES), :]
        hs.append((wa * _gelu(g[:, al * PEER_KEYS:(al + 1) * PEER_KEYS])).astype(BF16))
    h = jnp.concatenate(hs, axis=1)
    y_ref[...] += jnp.dot(h, v_ref[...], preferred_element_type=F32)


def _peer_ffn(xn, u, v, w, tm):
    t = xn.shape[0]
    ne = V7X_SUBLANES * PEER_KEYS
    return pl.pallas_call(
        _ffn_kernel,
        grid=(t // tm, u.shape[0] // ne),
        in_specs=[
            pl.BlockSpec((tm, D_MODEL), lambda i, n: (i, 0)),
            pl.BlockSpec((ne, D_MODEL), lambda i, n: (n, 0)),
            pl.BlockSpec((ne, D_MODEL), lambda i, n: (n, 0)),
            pl.BlockSpec((None, tm * V7X_SUBLANES, PEER_KEYS), lambda i, n: (n, i, 0)),
        ],
        out_specs=pl.BlockSpec((tm, D_MODEL), lambda i, n: (i, 0)),
        out_shape=jax.ShapeDtypeStruct((t, D_MODEL), F32),
        compiler_params=_params("parallel", "arbitrary"),
        name="peer_ffn",
    )(xn, u, v, w)


def _final_kernel(h_ref, y_ref, p_ref, g3_ref, wg_ref, wp_ref, gf_ref, o_ref):
    h = h_ref[...] + y_ref[...]
    xn = _rms_norm(h, g3_ref[...]).astype(BF16)
    gate = _sigmoid(jnp.dot(xn, wg_ref[...], preferred_element_type=F32))
    proj = jnp.dot(p_ref[...].astype(BF16), wp_ref[...], preferred_element_type=F32)
    o_ref[...] = _rms_norm(h + gate * proj, gf_ref[...])


def _final(h1, y, p2, g3, wg, wp, gf, tm):
    t = h1.shape[0]
    tok = lambda w: pl.BlockSpec((tm, w), lambda i: (i, 0))
    full = lambda a: pl.BlockSpec(a.shape, lambda i: (0, 0))
    return pl.pallas_call(
        _final_kernel,
        grid=(t // tm,),
        in_specs=[tok(D_MODEL), tok(D_MODEL), tok(PLE_DIM), full(g3), full(wg), full(wp), full(gf)],
        out_specs=tok(D_MODEL),
        out_shape=jax.ShapeDtypeStruct((t, D_MODEL), F32),
        compiler_params=_params("parallel"),
        name="ple_final",
    )(h1, y, p2, g3, wg, wp, gf)


_TM_PROJ = 512
_TM_SELECT = 128
_TM_FFN = 1024
_TM_FINAL = 1024


def _double_heads(w):
    d = w.shape[0]
    w = w.reshape(d, N_KV_HEADS, 1, HEAD_DIM)
    return jnp.broadcast_to(w, (d, N_KV_HEADS, 2, HEAD_DIM)).reshape(d, 2 * KV_W)


def kernel(x, p, norm_mix_g, w_in, attn_sink, conv_w, conv_b, lru_wa, lru_ba, lru_wx, lru_bx, lru_lambda,
           w_up_attn, w_up_lru, w_o, norm_ffn_g, peer_wq, peer_k1, peer_k2, peer_u, peer_v, norm_ple_g,
           ple_w_gate, ple_w_proj, final_g):
    bsz, seq, d = x.shape
    t = bsz * seq
    assert p.shape[0] == 1, "single-layer block: the final norm follows layer 0"
    h = x.reshape(t, d)
    for l in range(1):
        row = lambda a: a.reshape(1, -1)
        wi = w_in[l]
        w_comb = jnp.concatenate(
            [wi[:, :Q_W], _double_heads(wi[:, Q_W:Q_W + KV_W]), _double_heads(wi[:, Q_W + KV_W:Q_W + 2 * KV_W]),
             wi[:, Q_W + 2 * KV_W:]], axis=1).astype(BF16)
        q, kd, vd, lx, lg, ga, gl, u_bf16 = _in_proj(h, row(norm_mix_g[l]), w_comb, peer_u[l], _TM_PROJ)
        attn = _attention(attn_sink[l], q.reshape(bsz, seq, -1), kd.reshape(bsz, seq, -1),
                          vd.reshape(bsz, seq, -1)).reshape(t, Q_W)
        rec = _rg_lru(lx.reshape(bsz, seq, d), lg.reshape(bsz, seq, d), conv_w[l], row(conv_b[l]),
                      (0.5 * lru_wa[l]).astype(BF16), row(0.5 * lru_ba[l]),
                      (0.5 * lru_wx[l]).astype(BF16), row(0.5 * lru_bx[l]),
                      row(lru_lambda[l])).reshape(t, d)
        h1, xn2, qp, v_bf16 = _merge(attn, rec, ga, gl, h, w_up_attn[l].astype(BF16), w_up_lru[l].astype(BF16),
                                     w_o[l].astype(BF16), row(norm_ffn_g[l]), peer_wq[l].astype(BF16),
                                     peer_v[l], _TM_PROJ)
        w = _peer_select(qp, peer_k1[l].astype(BF16), peer_k2[l].astype(BF16), _TM_SELECT)
        y = _peer_ffn(xn2, u_bf16, v_bf16, w, _TM_FFN)
        h = _final(h1, y, p[l].reshape(t, PLE_DIM), row(norm_ple_g[l]), ple_w_gate[l].astype(BF16),
                   ple_w_proj[l].astype(BF16), row(final_g), _TM_FINAL)
    return h.reshape(bsz, seq, d)
```
